```python
import jax, jax.numpy as jnp
from jax import lax
import numpy as np

D_MODEL = 2048
BATCH = 2
SEQ = 4096
DEPTH = 4
DEC_BATCH = 32
DEC_SEQ = 8
PAST_LEN = 16384
PAGE_SIZE = 128

N_MIXERS = 2
N_CONV_LAYERS = (DEPTH + 1) // 2
N_ATTN_LAYERS = DEPTH // 2
CONV_WIDTH = 31
CONV_HIST = CONV_WIDTH - 1
CONV_INNER = D_MODEL
N_HEADS = 32
N_KV_HEADS = 4
HEAD_DIM = D_MODEL // N_HEADS
GROUP = N_HEADS // N_KV_HEADS
WINDOW = 128
BLOCK = WINDOW
D_FF = 7 * D_MODEL // 2
N_EXPERTS = 8
TOP_K = 2
D_FF_EXPERT = 7 * D_MODEL // 2
N_DENSE_FFN = (DEPTH + 1) // 2
N_MOE_FFN = DEPTH // 2
ALPHA = (2 * DEPTH) ** 0.25
BETA = (8 * DEPTH) ** -0.25
LN_EPS = 1e-5

kernel_name = "hybrid_conformer_conv_swa_sink_moe_decoder_step"


def layer_norm(x, g, b):
    xf = x.astype(jnp.float32)
    mu = jnp.mean(xf, -1, keepdims=True)
    var = jnp.mean(jnp.square(xf - mu), -1, keepdims=True)
    y = (xf - mu) * lax.rsqrt(var + LN_EPS) * g.astype(jnp.float32) + b.astype(jnp.float32)
    return y.astype(x.dtype)


def conv_module(x, hist, w_pw1, b_pw1, w_dw, b_dw, g_n, b_n, w_pw2, b_pw2):
    a, gate = jnp.split(x @ w_pw1 + b_pw1, 2, axis=-1)
    u = a * jax.nn.sigmoid(gate)
    u_ext = jnp.concatenate([hist.astype(u.dtype), u], axis=1)
    c = lax.conv_general_dilated(
        u_ext, w_dw[:, None, :], window_strides=(1,), padding='VALID',
        dimension_numbers=('NWC', 'WIO', 'NWC'), feature_group_count=CONV_INNER) + b_dw
    c = jax.nn.silu(layer_norm(c, g_n, b_n))
    y = c @ w_pw2 + b_pw2
    return y, u_ext[:, -CONV_HIST:]


def qkv_proj(x, w_qkv):
    B, L, _ = x.shape
    q, k, v = jnp.split(x @ w_qkv, [N_HEADS * HEAD_DIM, (N_HEADS + N_KV_HEADS) * HEAD_DIM], axis=-1)
    return (q.reshape(B, L, N_KV_HEADS, GROUP, HEAD_DIM),
            k.reshape(B, L, N_KV_HEADS, HEAD_DIM),
            v.reshape(B, L, N_KV_HEADS, HEAD_DIM))


def sink_attention(q, k, v, mask, sinks):
    s = jnp.einsum('bnqhgd,bnkhd->bnhgqk', q, k,
                   preferred_element_type=jnp.float32) * (HEAD_DIM ** -0.5)
    s = jnp.where(mask[None, :, None, None], s, -jnp.inf)
    sink = jnp.broadcast_to(sinks.astype(jnp.float32)[None, None, :, :, None, None],
                            s.shape[:-1] + (1,))
    p = jax.nn.softmax(jnp.concatenate([s, sink], axis=-1), axis=-1)[..., :-1]
    return jnp.einsum('bnhgqk,bnkhd->bnqhgd', p.astype(v.dtype), v)


def swa_prompt(x, w_qkv, sinks, w_o):
    B, L, _ = x.shape
    nb = L // BLOCK
    q, k, v = qkv_proj(x, w_qkv)
    qb = q.reshape(B, nb, BLOCK, N_KV_HEADS, GROUP, HEAD_DIM)
    kb = k.reshape(B, nb, BLOCK, N_KV_HEADS, HEAD_DIM)
    vb = v.reshape(B, nb, BLOCK, N_KV_HEADS, HEAD_DIM)
    pad = ((0, 0), (1, 0), (0, 0), (0, 0), (0, 0))
    k_band = jnp.concatenate([jnp.pad(kb, pad)[:, :-1], kb], axis=2)
    v_band = jnp.concatenate([jnp.pad(vb, pad)[:, :-1], vb], axis=2)
    qpos = jnp.arange(nb)[:, None] * BLOCK + jnp.arange(BLOCK)[None, :]
    kpos = jnp.arange(nb)[:, None] * BLOCK - BLOCK + jnp.arange(2 * BLOCK)[None, :]
    rel = qpos[:, :, None] - kpos[:, None, :]
    mask = (rel >= 0) & (rel <= WINDOW) & (kpos[:, None, :] >= 0)
    o = sink_attention(qb, k_band, v_band, mask, sinks.reshape(N_KV_HEADS, GROUP))
    y = o.reshape(B, L, N_HEADS * HEAD_DIM) @ w_o
    return y, k[:, -WINDOW:], v[:, -WINDOW:]


def swa_sample(x, k_buf, v_buf, w_qkv, sinks, w_o):
    B, T, _ = x.shape
    W = k_buf.shape[1]
    q, k, v = qkv_proj(x, w_qkv)
    k_all = jnp.concatenate([k_buf.astype(k.dtype), k], axis=1)
    v_all = jnp.concatenate([v_buf.astype(v.dtype), v], axis=1)
    rel = (jnp.arange(T) + W)[:, None] - jnp.arange(W + T)[None, :]
    mask = ((rel >= 0) & (rel <= WINDOW))[None]
    o = sink_attention(q[:, None], k_all[:, None], v_all[:, None], mask,
                       sinks.reshape(N_KV_HEADS, GROUP))[:, 0]
    y = o.reshape(B, T, N_HEADS * HEAD_DIM) @ w_o
    return y, k_all[:, -W:], v_all[:, -W:]


def swiglu(x, w_gu, w_down):
    g, u = jnp.split(x @ w_gu, 2, axis=-1)
    return (jax.nn.silu(g) * u) @ w_down


def moe_ffn(x, w_router, w_e_gu, w_e_down):
    shp = x.shape
    xt = x.reshape(-1, D_MODEL)
    logits = jnp.einsum('nd,de->ne', xt, w_router, preferred_element_type=jnp.float32)
    top_v, top_i = lax.top_k(logits, TOP_K)
    top_w = jax.nn.softmax(top_v, axis=-1)
    gates = jnp.einsum('nk,nke->ne', top_w,
                       jax.nn.one_hot(top_i, N_EXPERTS, dtype=jnp.float32)).astype(xt.dtype)
    y = jnp.zeros_like(xt)
    for e in range(N_EXPERTS):
        y = y + gates[:, e:e + 1] * swiglu(xt, w_e_gu[e], w_e_down[e])
    return y.reshape(shp)


def trunk(x, conv_hist, k_buf, v_buf, ln1_g, ln1_b, ln2_g, ln2_b,
          w_pw1, b_pw1, w_dw, b_dw, conv_norm_g, conv_norm_b, w_pw2, b_pw2,
          w_qkv, attn_sinks, w_o, w_ffn_gu, w_ffn_down, w_router, w_exp_gu, w_exp_down):
    new_conv, new_k, new_v = [], [], []
    for i in range(DEPTH):
        j = i // N_MIXERS
        if i % N_MIXERS == 0:
            hist = (jnp.zeros((x.shape[0], CONV_HIST, CONV_INNER), x.dtype)
                    if conv_hist is None else conv_hist[j])
            h, nh = conv_module(x, hist, w_pw1[j], b_pw1[j], w_dw[j], b_dw[j],
                                conv_norm_g[j], conv_norm_b[j], w_pw2[j], b_pw2[j])
            new_conv.append(nh)
        else:
            if k_buf is None:
                h, nk, nv = swa_prompt(x, w_qkv[j], attn_sinks[j], w_o[j])
            else:
                h, nk, nv = swa_sample(x, k_buf[j], v_buf[j], w_qkv[j], attn_sinks[j], w_o[j])
            new_k.append(nk)
            new_v.append(nv)
        x = layer_norm(ALPHA * x + h, ln1_g[i], ln1_b[i])
        if i % 2 == 0:
            f = swiglu(x, w_ffn_gu[i // 2], w_ffn_down[i // 2])
        else:
            f = moe_ffn(x, w_router[i // 2], w_exp_gu[i // 2], w_exp_down[i // 2])
        x = layer_norm(ALPHA * x + f, ln2_g[i], ln2_b[i])
    return x, jnp.stack(new_conv), jnp.stack(new_k), jnp.stack(new_v)


def setup_inputs(seed: int = 0) -> dict:
    key = jax.random.key(seed)
    ks = iter(jax.random.split(key, 32))
    f32 = jnp.float32
    nrm = lambda shape, scale: jax.random.normal(next(ks), shape, f32) * scale
    gain = lambda shape: 1.0 + nrm(shape, 0.02)
    qkv_w = (N_HEADS + 2 * N_KV_HEADS) * HEAD_DIM
    win_buf = min(WINDOW, PAST_LEN)
    return {
        "x_prompt": nrm((BATCH, SEQ, D_MODEL), 1.0),
        "x_sample": nrm((DEC_BATCH, DEC_SEQ, D_MODEL), 1.0),
        "state_conv": nrm((N_CONV_LAYERS, DEC_BATCH, CONV_HIST, CONV_INNER), 0.5),
        "cache_k": nrm((N_ATTN_LAYERS, DEC_BATCH, win_buf, N_KV_HEADS, HEAD_DIM), 1.0),
        "cache_v": nrm((N_ATTN_LAYERS, DEC_BATCH, win_buf, N_KV_HEADS, HEAD_DIM), 1.0),
        "ln1_g": gain((DEPTH, D_MODEL)),
        "ln1_b": nrm((DEPTH, D_MODEL), 0.02),
        "ln2_g": gain((DEPTH, D_MODEL)),
        "ln2_b": nrm((DEPTH, D_MODEL), 0.02),
        "w_pw1": nrm((N_CONV_LAYERS, D_MODEL, 2 * CONV_INNER), D_MODEL ** -0.5),
        "b_pw1": nrm((N_CONV_LAYERS, 2 * CONV_INNER), 0.02),
        "w_dw": nrm((N_CONV_LAYERS, CONV_WIDTH, CONV_INNER), CONV_WIDTH ** -0.5),
        "b_dw": nrm((N_CONV_LAYERS, CONV_INNER), 0.02),
        "conv_norm_g": gain((N_CONV_LAYERS, CONV_INNER)),
        "conv_norm_b": nrm((N_CONV_LAYERS, CONV_INNER), 0.02),
        "w_pw2": nrm((N_CONV_LAYERS, CONV_INNER, D_MODEL), CONV_INNER ** -0.5 * BETA),
        "b_pw2": nrm((N_CONV_LAYERS, D_MODEL), 0.02),
        "w_qkv": nrm((N_ATTN_LAYERS, D_MODEL, qkv_w), D_MODEL ** -0.5),
        "attn_sinks": nrm((N_ATTN_LAYERS, N_HEADS), 1.0),
        "w_o": nrm((N_ATTN_LAYERS, N_HEADS * HEAD_DIM, D_MODEL), (N_HEADS * HEAD_DIM) ** -0.5 * BETA),
        "w_ffn_gu": nrm((N_DENSE_FFN, D_MODEL, 2 * D_FF), D_MODEL ** -0.5),
        "w_ffn_down": nrm((N_DENSE_FFN, D_FF, D_MODEL), D_FF ** -0.5 * BETA),
        "w_router": nrm((N_MOE_FFN, D_MODEL, N_EXPERTS), D_MODEL ** -0.5),
        "w_exp_gu": nrm((N_MOE_FFN, N_EXPERTS, D_MODEL, 2 * D_FF_EXPERT), D_MODEL ** -0.5),
        "w_exp_down": nrm((N_MOE_FFN, N_EXPERTS, D_FF_EXPERT, D_MODEL), D_FF_EXPERT ** -0.5 * BETA),
    }


def reference(x_prompt, x_sample, state_conv, cache_k, cache_v, ln1_g, ln1_b, ln2_g, ln2_b,
              w_pw1, b_pw1, w_dw, b_dw, conv_norm_g, conv_norm_b, w_pw2, b_pw2,
              w_qkv, attn_sinks, w_o, w_ffn_gu, w_ffn_down, w_router, w_exp_gu, w_exp_down):
    y_prompt, conv_p, k_p, v_p = trunk(
        x_prompt, None, None, None, ln1_g, ln1_b, ln2_g, ln2_b,
        w_pw1, b_pw1, w_dw, b_dw, conv_norm_g, conv_norm_b, w_pw2, b_pw2,
        w_qkv, attn_sinks, w_o, w_ffn_gu, w_ffn_down, w_router, w_exp_gu, w_exp_down)
    y_sample, conv_s, k_s, v_s = trunk(
        x_sample, state_conv, cache_k, cache_v, ln1_g, ln1_b, ln2_g, ln2_b,
        w_pw1, b_pw1, w_dw, b_dw, conv_norm_g, conv_norm_b, w_pw2, b_pw2,
        w_qkv, attn_sinks, w_o, w_ffn_gu, w_ffn_down, w_router, w_exp_gu, w_exp_down)
    return (y_prompt, y_sample, conv_p, conv_s, k_p, v_p, k_s, v_s)
```

```python
import functools

import jax
import jax.numpy as jnp
from jax import lax
from jax.experimental import pallas as pl
from jax.experimental.pallas import tpu as pltpu

F32 = jnp.float32
BF16 = jnp.bfloat16

N_HEADS = 32
N_KV_HEADS = 4
GROUP = N_HEADS // N_KV_HEADS
WINDOW = 128
CONV_WIDTH = 31
CONV_HIST = CONV_WIDTH - 1
N_EXPERTS = 8
LN_EPS = 1e-5

V7X_LANES = 128
V7X_SUBLANES = 8
V7X_BF16_ROWS = 16
V7X_VMEM_LIMIT = 58 * 1024 * 1024

MM_ROWS_CAP = 1056
MM1_COLS = 512
MM2_K = 512
LN_ROWS_CAP = 264
MOE_CHUNK = 2560
MOE_ROWBLK = 1280
MOE_SUB = 256
MOE_ALIGN = 128
MOE_COLS = 512
MOE_K = 512
ROW_SIZES_CHUNK = (1024, 1024, 512, 256, 128)
ROW_SIZES_ROWBLK = (1024, 512, 256, 128)
CONV_ROWS = 256
CONV_HALO = 32
CONV_COLS = 256
CONV_ROW_CHUNK = 32
COMBINE_ROWS = 256
NEG_INF = float("-inf")


def _pick_tile(n, cap, mult):
    best = None
    for d in range(mult, min(n, cap) + 1, mult):
        if n % d == 0:
            best = d
    if best is None:
        raise ValueError(f"no tile for {n} (cap {cap}, multiple of {mult})")
    return best


def _params(semantics):
    return pltpu.CompilerParams(dimension_semantics=semantics,
                                vmem_limit_bytes=V7X_VMEM_LIMIT)


def _silu(x):
    return x * jax.nn.sigmoid(x)


def _layer_norm_rows(z, g, b):
    mu = jnp.mean(z, axis=-1, keepdims=True)
    d = z - mu
    var = jnp.mean(d * d, axis=-1, keepdims=True)
    return d * lax.rsqrt(var + LN_EPS) * g + b


def _mm1_pair_kernel(*refs, mode, has_bias):
    if has_bias:
        x_ref, wa_ref, wb_ref, ba_ref, bb_ref, o_ref = refs
    else:
        x_ref, wa_ref, wb_ref, o_ref = refs
    x = x_ref[...]
    a = jnp.dot(x, wa_ref[...].astype(BF16), preferred_element_type=F32)
    b = jnp.dot(x, wb_ref[...].astype(BF16), preferred_element_type=F32)
    if has_bias:
        a = a + ba_ref[...]
        b = b + bb_ref[...]
    if mode == "glu":
        r = a * jax.nn.sigmoid(b)
    else:
        r = _silu(a) * b
    o_ref[...] = r.astype(o_ref.dtype)


def _mm1_pair(x, w, bias, mode, out_dtype, name):
    m, k = x.shape
    h = w.shape[1] // 2
    tm = _pick_tile(m, MM_ROWS_CAP, V7X_BF16_ROWS)
    tn = _pick_tile(h, MM1_COLS, V7X_LANES)
    nj = h // tn
    in_specs = [
        pl.BlockSpec((tm, k), lambda i, j: (i, 0)),
        pl.BlockSpec((k, tn), lambda i, j: (0, j)),
        pl.BlockSpec((k, tn), lambda i, j: (0, j + nj)),
    ]
    args = [x, w, w]
    if bias is not None:
        b2 = bias.reshape(1, 2 * h)
        in_specs += [pl.BlockSpec((1, tn), lambda i, j: (0, j)),
                     pl.BlockSpec((1, tn), lambda i, j: (0, j + nj))]
        args += [b2, b2]
    return pl.pallas_call(
        functools.partial(_mm1_pair_kernel, mode=mode, has_bias=bias is not None),
        out_shape=jax.ShapeDtypeStruct((m, h), out_dtype),
        grid=(m // tm, nj),
        in_specs=in_specs,
        out_specs=pl.BlockSpec((tm, tn), lambda i, j: (i, j)),
        compiler_params=_params(("parallel", "arbitrary")),
        name=name,
    )(*args)


def _mm1_plain_kernel(x_ref, w_ref, o_ref):
    o_ref[...] = jnp.dot(x_ref[...], w_ref[...].astype(BF16),
                         preferred_element_type=F32).astype(o_ref.dtype)


def _mm1_plain(x, w, out_dtype, name):
    m, k = x.shape
    n = w.shape[1]
    tm = _pick_tile(m, MM_ROWS_CAP, V7X_BF16_ROWS)
    tn = _pick_tile(n, MM1_COLS, V7X_LANES)
    return pl.pallas_call(
        _mm1_plain_kernel,
        out_shape=jax.ShapeDtypeStruct((m, n), out_dtype),
        grid=(m // tm, n // tn),
        in_specs=[pl.BlockSpec((tm, k), lambda i, j: (i, 0)),
                  pl.BlockSpec((k, tn), lambda i, j: (0, j))],
        out_specs=pl.BlockSpec((tm, tn), lambda i, j: (i, j)),
        compiler_params=_params(("parallel", "arbitrary")),
        name=name,
    )(x, w)


def _route_from_logits(logits):
    lane = lax.broadcasted_iota(jnp.int32, logits.shape, 1).astype(F32)
    no_lane = float(V7X_LANES)
    lg = jnp.where(lane < N_EXPERTS, logits, NEG_INF)
    m1 = jnp.max(lg, axis=-1, keepdims=True)
    i1 = jnp.min(jnp.where(lg == m1, lane, no_lane), axis=-1, keepdims=True)
    lg2 = jnp.where(lane == i1, NEG_INF, lg)
    m2 = jnp.max(lg2, axis=-1, keepdims=True)
    i2 = jnp.min(jnp.where(lg2 == m2, lane, no_lane), axis=-1, keepdims=True)
    e2 = jnp.exp(m2 - m1)
    den = 1.0 + e2
    w1 = 1.0 / den
    w2 = e2 / den
    out = jnp.where(lane == 0.0, i1,
                    jnp.where(lane == 1.0, i2,
                              jnp.where(lane == 2.0, w1,
                                        jnp.where(lane == 3.0, w2, 0.0))))
    return out


def _split_bf16(v):
    hi = v.astype(BF16)
    lo = (v - hi.astype(F32)).astype(BF16)
    return hi, lo


def _mm2_kernel(*refs, alpha, has_bias, has_router, row_half, ln_rows):
    refs = list(refs)
    h_ref, w_ref, r_ref, g_ref, b_ref = refs[:5]
    pos = 5
    bias_ref = None
    wr_ref = None
    if has_bias:
        bias_ref = refs[pos]
        pos += 1
    if has_router:
        wr_ref = refs[pos]
        pos += 1
    y_ref, yb_ref = refs[pos], refs[pos + 1]
    pos += 2
    route_ref = refs[pos] if has_router else None

    kk = pl.program_id(1)
    tm = y_ref.shape[0]

    @pl.when(kk == 0)
    def _():
        z0 = alpha * r_ref[...]
        if has_bias:
            z0 = z0 + bias_ref[...]
        y_ref[...] = z0

    wb = w_ref[...].astype(BF16)
    for s in range(tm // row_half):
        rows = pl.ds(s * row_half, row_half)
        y_ref[rows, :] += jnp.dot(h_ref[rows, :], wb, preferred_element_type=F32)

    @pl.when(kk == pl.num_programs(1) - 1)
    def _():
        g = g_ref[...]
        b = b_ref[...]
        if has_router:
            wr = wr_ref[...]
            wr_hi, wr_lo = _split_bf16(wr)

        def ln_chunk(c, carry):
            rows = pl.ds(pl.multiple_of(c * ln_rows, V7X_SUBLANES), ln_rows)
            y = _layer_norm_rows(y_ref[rows, :], g, b)
            y_ref[rows, :] = y
            yb_ref[rows, :] = y.astype(BF16)
            if has_router:
                y_hi, y_lo = _split_bf16(y)
                logits = (jnp.dot(y_hi, wr_hi, preferred_element_type=F32)
                          + jnp.dot(y_hi, wr_lo, preferred_element_type=F32)
                          + jnp.dot(y_lo, wr_hi, preferred_element_type=F32))
                route_ref[rows, :] = _route_from_logits(logits)
            return carry

        lax.fori_loop(0, tm // ln_rows, ln_chunk, 0)


def _mm2(h, w, bias, resid, ln_g, ln_b, alpha, w_router, name):
    m, k = h.shape
    d = w.shape[1]
    tm = _pick_tile(m, MM_ROWS_CAP, V7X_BF16_ROWS)
    tk = _pick_tile(k, MM2_K, V7X_LANES)
    row_half = tm // 2 if (tm // 2) % V7X_BF16_ROWS == 0 else tm
    ln_rows = _pick_tile(tm, LN_ROWS_CAP, V7X_SUBLANES)
    once = pl.Buffered(1)
    in_specs = [
        pl.BlockSpec((tm, tk), lambda i, kk: (i, kk)),
        pl.BlockSpec((tk, d), lambda i, kk: (kk, 0)),
        pl.BlockSpec((tm, d), lambda i, kk: (i, 0), pipeline_mode=once),
        pl.BlockSpec((1, d), lambda i, kk: (0, 0)),
        pl.BlockSpec((1, d), lambda i, kk: (0, 0)),
    ]
    args = [h, w, resid, ln_g.reshape(1, d), ln_b.reshape(1, d)]
    if bias is not None:
        in_specs.append(pl.BlockSpec((1, d), lambda i, kk: (0, 0)))
        args.append(bias.reshape(1, d))
    out_shape = [jax.ShapeDtypeStruct((m, d), F32), jax.ShapeDtypeStruct((m, d), BF16)]
    out_specs = [pl.BlockSpec((tm, d), lambda i, kk: (i, 0)),
                 pl.BlockSpec((tm, d), lambda i, kk: (i, 0))]
    if w_router is not None:
        wr = jnp.zeros((d, V7X_LANES), F32).at[:, :w_router.shape[1]].set(w_router)
        in_specs.append(pl.BlockSpec((d, V7X_LANES), lambda i, kk: (0, 0)))
        args.append(wr)
        out_shape.append(jax.ShapeDtypeStruct((m, V7X_LANES), F32))
        out_specs.append(pl.BlockSpec((tm, V7X_LANES), lambda i, kk: (i, 0)))
    return pl.pallas_call(
        functools.partial(_mm2_kernel, alpha=alpha, has_bias=bias is not None,
                          has_router=w_router is not None, row_half=row_half,
                          ln_rows=ln_rows),
        out_shape=out_shape,
        grid=(m // tm, k // tk),
        in_specs=in_specs,
        out_specs=out_specs,
        compiler_params=_params(("parallel", "arbitrary")),
        name=name,
    )(*args)


def _conv_ln_silu(ext_ref, pre_ref, w_ref, bdw_ref, g_ref, b_ref, o_ref, t_rows):
    c_total = pre_ref.shape[1]
    cols_blk = min(CONV_COLS, c_total)
    row_chunk = min(CONV_ROW_CHUNK, t_rows)
    lead = CONV_HALO - CONV_HIST

    def col_body(c, carry):
        cols = pl.ds(pl.multiple_of(c * cols_blk, V7X_LANES), cols_blk)
        bdw = bdw_ref[:, cols]
        for r in range(t_rows // row_chunk):
            acc = jnp.broadcast_to(bdw, (row_chunk, cols_blk))
            for k in range(CONV_WIDTH):
                acc = acc + w_ref[pl.ds(k, 1), cols] * ext_ref[
                    pl.ds(r * row_chunk + lead + k, row_chunk), cols]
            pre_ref[pl.ds(r * row_chunk, row_chunk), cols] = acc
        return carry

    lax.fori_loop(0, c_total // cols_blk, col_body, 0)
    y = _layer_norm_rows(pre_ref[...], g_ref[...], b_ref[...])
    o_ref[...] = _silu(y).astype(o_ref.dtype)


def _conv_prompt_kernel(prev_ref, cur_ref, w_ref, bdw_ref, g_ref, b_ref, o_ref,
                        ext_ref, pre_ref):
    i = pl.program_id(1)
    t_rows = cur_ref.shape[0]
    prev = prev_ref[...]
    ext_ref[pl.ds(0, CONV_HALO), :] = jnp.where(i == 0, jnp.zeros_like(prev), prev)
    ext_ref[pl.ds(CONV_HALO, t_rows), :] = cur_ref[...]
    _conv_ln_silu(ext_ref, pre_ref, w_ref, bdw_ref, g_ref, b_ref, o_ref, t_rows)


def _conv_prompt(u, batch, seq, w_dw, b_dw, g, b):
    c = u.shape[1]
    t = _pick_tile(seq, CONV_ROWS, CONV_HALO)
    nt = seq // t
    halo_per_t = t // CONV_HALO

    def prev_map(bi, i):
        return (jnp.maximum(bi * (seq // CONV_HALO) + i * halo_per_t - 1, 0), 0)

    vec = lambda a: a.reshape(1, c)
    return pl.pallas_call(
        _conv_prompt_kernel,
        out_shape=jax.ShapeDtypeStruct((batch * seq, c), BF16),
        grid=(batch, nt),
        in_specs=[
            pl.BlockSpec((CONV_HALO, c), prev_map),
            pl.BlockSpec((t, c), lambda bi, i: (bi * nt + i, 0)),
            pl.BlockSpec((CONV_WIDTH, c), lambda bi, i: (0, 0)),
            pl.BlockSpec((1, c), lambda bi, i: (0, 0)),
            pl.BlockSpec((1, c), lambda bi, i: (0, 0)),
            pl.BlockSpec((1, c), lambda bi, i: (0, 0)),
        ],
        out_specs=pl.BlockSpec((t, c), lambda bi, i: (bi * nt + i, 0)),
        scratch_shapes=[pltpu.VMEM((t + CONV_HALO, c), F32), pltpu.VMEM((t, c), F32)],
        compiler_params=_params(("parallel", "arbitrary")),
        name="conv_prompt",
    )(u, u, w_dw, vec(b_dw), vec(g), vec(b))


def _conv_sample_kernel(ext_ref, w_ref, bdw_ref, g_ref, b_ref, o_ref, pre_ref):
    _conv_ln_silu(ext_ref, pre_ref, w_ref, bdw_ref, g_ref, b_ref, o_ref, o_ref.shape[0])


def _conv_sample(ext, w_dw, b_dw, g, b):
    s, rows, c = ext.shape
    t = rows - CONV_HIST
    lead = CONV_HALO - CONV_HIST
    ext = jnp.pad(ext, ((0, 0), (lead, 0), (0, 0)))
    vec = lambda a: a.reshape(1, c)
    return pl.pallas_call(
        _conv_sample_kernel,
        out_shape=jax.ShapeDtypeStruct((s * t, c), BF16),
        grid=(s,),
        in_specs=[
            pl.BlockSpec((None, rows + lead, c), lambda i: (i, 0, 0)),
            pl.BlockSpec((CONV_WIDTH, c), lambda i: (0, 0)),
            pl.BlockSpec((1, c), lambda i: (0, 0)),
            pl.BlockSpec((1, c), lambda i: (0, 0)),
            pl.BlockSpec((1, c), lambda i: (0, 0)),
        ],
        out_specs=pl.BlockSpec((t, c), lambda i: (i, 0)),
        scratch_shapes=[pltpu.VMEM((t, c), F32)],
        compiler_params=_params(("parallel",)),
        name="conv_sample",
    )(ext, w_dw, vec(b_dw), vec(g), vec(b))


def _attend(q, k, v, mask, sink, scale):
    s = lax.dot_general(q, k, (((1,), (1,)), ((), ())),
                        preferred_element_type=F32) * scale
    s = jnp.where(mask, s, NEG_INF)
    m = jnp.maximum(jnp.max(s, axis=-1, keepdims=True), sink)
    p = jnp.exp(s - m)
    den = jnp.sum(p, axis=-1, keepdims=True) + jnp.exp(sink - m)
    o = jnp.dot(p.astype(BF16), v, preferred_element_type=F32)
    return o / den


def _attn_prompt_kernel(sink_ref, q_ref, kp_ref, kc_ref, vp_ref, vc_ref, o_ref, *, hd):
    n = pl.program_id(1)
    blk = q_ref.shape[0]
    scale = hd ** -0.5
    qi = lax.broadcasted_iota(jnp.int32, (blk, 2 * blk), 0)
    ci = lax.broadcasted_iota(jnp.int32, (blk, 2 * blk), 1)
    mask = (ci >= qi) & (ci <= qi + WINDOW) & ((ci >= blk) | (n > 0))
    kband = jnp.concatenate([kp_ref[...], kc_ref[...]], axis=0).astype(BF16)
    vband = jnp.concatenate([vp_ref[...], vc_ref[...]], axis=0).astype(BF16)
    for h in range(N_KV_HEADS):
        kh = kband[:, h * hd:(h + 1) * hd]
        vh = vband[:, h * hd:(h + 1) * hd]
        for g in range(GROUP):
            head = h * GROUP + g
            qh = q_ref[:, head * hd:(head + 1) * hd].astype(BF16)
            o = _attend(qh, kh, vh, mask, sink_ref[head], scale)
            o_ref[:, head * hd:(head + 1) * hd] = o.astype(o_ref.dtype)


def _attn_prompt(qkv, sinks, batch, seq, d_model):
    hd = d_model // N_HEADS
    kvw = N_KV_HEADS * hd
    blk = WINDOW
    nb = seq // blk
    qcols = d_model // kvw

    def cur(off):
        return lambda bi, n, s: (bi * nb + n, off)

    def prev(off):
        return lambda bi, n, s: (bi * nb + jnp.maximum(n - 1, 0), off)

    grid_spec = pltpu.PrefetchScalarGridSpec(
        num_scalar_prefetch=1,
        grid=(batch, nb),
        in_specs=[
            pl.BlockSpec((blk, d_model), lambda bi, n, s: (bi * nb + n, 0)),
            pl.BlockSpec((blk, kvw), prev(qcols)),
            pl.BlockSpec((blk, kvw), cur(qcols)),
            pl.BlockSpec((blk, kvw), prev(qcols + 1)),
            pl.BlockSpec((blk, kvw), cur(qcols + 1)),
        ],
        out_specs=pl.BlockSpec((blk, d_model), lambda bi, n, s: (bi * nb + n, 0)),
    )
    return pl.pallas_call(
        functools.partial(_attn_prompt_kernel, hd=hd),
        out_shape=jax.ShapeDtypeStruct((batch * seq, d_model), BF16),
        grid_spec=grid_spec,
        compiler_params=_params(("parallel", "arbitrary")),
        name="attn_prompt",
    )(sinks, qkv, qkv, qkv, qkv, qkv)


def _attn_sample_kernel(sink_ref, q_ref, kc_ref, kn_ref, vc_ref, vn_ref, o_ref, *, hd, t_new):
    scale = hd ** -0.5
    w_buf = kc_ref.shape[1]
    rows = GROUP * t_new
    s_len = w_buf + t_new
    ri = lax.broadcasted_iota(jnp.int32, (rows, s_len), 0)
    ci = lax.broadcasted_iota(jnp.int32, (rows, s_len), 1)
    ti = ri % t_new
    mask = (ci >= ti + (w_buf - WINDOW)) & (ci <= ti + w_buf)
    gi = lax.broadcasted_iota(jnp.int32, (rows, 1), 0) // t_new
    for h in range(N_KV_HEADS):
        k = jnp.concatenate([kc_ref[h], kn_ref[h]], axis=0).astype(BF16)
        v = jnp.concatenate([vc_ref[h], vn_ref[h]], axis=0).astype(BF16)
        sink = jnp.zeros((rows, 1), F32)
        for g in range(GROUP):
            sink = jnp.where(gi == g, sink_ref[h * GROUP + g], sink)
        o = _attend(q_ref[h].astype(BF16), k, v, mask, sink, scale)
        o_ref[h] = o.astype(o_ref.dtype)


def _attn_sample(q, k_new, v_new, k_buf, v_buf, sinks):
    s, t_new, d_model = q.shape
    hd = d_model // N_HEADS
    w_buf = k_buf.shape[1]
    rows = GROUP * t_new
    qh = q.reshape(s, t_new, N_KV_HEADS, GROUP, hd).transpose(0, 2, 3, 1, 4)
    qh = qh.reshape(s, N_KV_HEADS, rows, hd)
    heads_first = lambda a: a.transpose(0, 2, 1, 3)
    blk = lambda r: pl.BlockSpec((None, N_KV_HEADS, r, hd), lambda i, sk: (i, 0, 0, 0))
    grid_spec = pltpu.PrefetchScalarGridSpec(
        num_scalar_prefetch=1,
        grid=(s,),
        in_specs=[blk(rows), blk(w_buf), blk(t_new), blk(w_buf), blk(t_new)],
        out_specs=blk(rows),
    )
    o = pl.pallas_call(
        functools.partial(_attn_sample_kernel, hd=hd, t_new=t_new),
        out_shape=jax.ShapeDtypeStruct((s, N_KV_HEADS, rows, hd), BF16),
        grid_spec=grid_spec,
        compiler_params=_params(("parallel",)),
        name="attn_sample",
    )(sinks, qh, heads_first(k_buf), heads_first(k_new), heads_first(v_buf),
      heads_first(v_new))
    o = o.reshape(s, N_KV_HEADS, GROUP, t_new, hd).transpose(0, 3, 1, 2, 4)
    return o.reshape(s * t_new, d_model)


def _compact(active, size):
    order = jnp.argsort(jnp.logical_not(active).astype(jnp.int32), stable=True)
    order = order[:size].astype(jnp.int32)
    return order, jnp.sum(active).astype(jnp.int32)


def _route_plan(i1, i2, n_tok):
    n_asg = 2 * n_tok
    halves = MOE_CHUNK // MOE_ROWBLK
    subs = MOE_CHUNK // MOE_SUB
    w_max = n_asg // MOE_CHUNK + N_EXPERTS
    r_max = n_asg // MOE_ROWBLK + N_EXPERTS
    s_max = n_asg // MOE_SUB + N_EXPERTS

    ei = jnp.stack([i1, i2], axis=1).reshape(-1)
    onehot = (ei[:, None] == jnp.arange(N_EXPERTS, dtype=jnp.int32)[None, :]).astype(jnp.int32)
    csum = jnp.cumsum(onehot, axis=0)
    counts = csum[-1]
    rank = jnp.take_along_axis(csum, ei[:, None], axis=1)[:, 0] - 1
    nch = (counts + MOE_CHUNK - 1) // MOE_CHUNK
    cend = jnp.cumsum(nch)
    cstart = cend - nch
    slot = (cstart[ei] * MOE_CHUNK + rank).astype(jnp.int32)
    n_chunks = cend[-1]

    wid = jnp.arange(w_max, dtype=jnp.int32)
    wcl = jnp.minimum(wid, n_chunks - 1)
    ce = jnp.sum((cend[None, :] <= wcl[:, None]).astype(jnp.int32), axis=1)
    ce = jnp.minimum(ce, N_EXPERTS - 1).astype(jnp.int32)
    crow = jnp.clip(counts[ce] - (wcl - cstart[ce]) * MOE_CHUNK, 0, MOE_CHUNK)
    crow = jnp.where(wid < n_chunks, crow, 0).astype(jnp.int32)

    def round_up(v, mlt):
        return ((v + mlt - 1) // mlt) * mlt

    rb_rows = jnp.clip(crow[:, None] - jnp.arange(halves, dtype=jnp.int32)[None, :] * MOE_ROWBLK,
                       0, MOE_ROWBLK).reshape(-1)
    rb_ids, n_rb = _compact(rb_rows > 0, r_max)
    rb_ids = rb_ids[jnp.minimum(jnp.arange(r_max), n_rb - 1)]
    rb_n = jnp.where(jnp.arange(r_max) < n_rb, round_up(rb_rows[rb_ids], MOE_ALIGN), 0)
    rb_e = ce[rb_ids // halves]

    sb_rows = jnp.clip(crow[:, None] - jnp.arange(subs, dtype=jnp.int32)[None, :] * MOE_SUB,
                       0, MOE_SUB).reshape(-1)
    sb_ids, n_sb = _compact(sb_rows > 0, s_max)
    sb_ids = sb_ids[jnp.minimum(jnp.arange(s_max), n_sb - 1)]
    src = jnp.zeros((w_max * MOE_CHUNK,), jnp.int32).at[slot].set(
        jnp.arange(n_asg, dtype=jnp.int32) // 2)
    src_sb = src.reshape(w_max * subs, MOE_SUB)[sb_ids]

    return dict(
        slot=slot, w_max=w_max,
        chunk_e=ce, chunk_blk=wcl.astype(jnp.int32),
        chunk_n=round_up(crow, MOE_ALIGN).astype(jnp.int32),
        rb_e=rb_e.astype(jnp.int32), rb_blk=rb_ids.astype(jnp.int32),
        rb_n=rb_n.astype(jnp.int32),
        sb_blk=sb_ids.astype(jnp.int32), n_sb=n_sb.reshape(1),
        src_sb=src_sb.reshape(s_max, 1, MOE_SUB).astype(jnp.int32),
    )


def _row_copy(src_hbm, row, dst_ref, dst_row, sem):
    return pltpu.make_async_copy(src_hbm.at[pl.ds(row, 1)], dst_ref.at[pl.ds(dst_row, 1)], sem)


def _gather_kernel(sb_ref, nsb_ref, src_ref, x_hbm, o_ref, buf_ref, sem):
    i = pl.program_id(0)
    n_rows = buf_ref.shape[0]

    @pl.when(i < nsb_ref[0])
    def _():
        def issue(r, carry):
            _row_copy(x_hbm, src_ref[0, 0, r], buf_ref, r, sem).start()
            return carry

        def wait(r, carry):
            _row_copy(x_hbm, 0, buf_ref, r, sem).wait()
            return carry

        lax.fori_loop(0, n_rows, issue, 0)
        lax.fori_loop(0, n_rows, wait, 0)
        o_ref[...] = buf_ref[...].astype(o_ref.dtype)


def _moe_gather(x, plan):
    d = x.shape[1]
    s_max = plan["sb_blk"].shape[0]
    grid_spec = pltpu.PrefetchScalarGridSpec(
        num_scalar_prefetch=2,
        grid=(s_max,),
        in_specs=[
            pl.BlockSpec((1, 1, MOE_SUB), lambda i, sb, ns: (i, 0, 0),
                         memory_space=pltpu.SMEM),
            pl.BlockSpec(memory_space=pl.ANY),
        ],
        out_specs=pl.BlockSpec((MOE_SUB, d), lambda i, sb, ns: (sb[i], 0)),
        scratch_shapes=[pltpu.VMEM((MOE_SUB, d), F32), pltpu.SemaphoreType.DMA(())],
    )
    return pl.pallas_call(
        _gather_kernel,
        out_shape=jax.ShapeDtypeStruct((plan["w_max"] * MOE_CHUNK, d), BF16),
        grid_spec=grid_spec,
        compiler_params=_params(("arbitrary",)),
        name="moe_gather",
    )(plan["sb_blk"], plan["n_sb"], plan["src_sb"], x)


def _for_each_row_block(n_rows, sizes, fn):
    off = jnp.int32(0)
    for size in sizes:
        take = (n_rows - off) >= size

        @pl.when(take)
        def _(off=off, size=size):
            fn(pl.multiple_of(off, MOE_ALIGN), size)

        off = off + jnp.where(take, size, 0)


def _gmm1_kernel(ce_ref, cb_ref, cn_ref, x_ref, wg_ref, wu_ref, o_ref, wgb_ref, wub_ref):
    n = cn_ref[pl.program_id(0)]

    @pl.when(n > 0)
    def _():
        wgb_ref[...] = wg_ref[...].astype(BF16)
        wub_ref[...] = wu_ref[...].astype(BF16)

        def blk(off, size):
            rows = pl.ds(off, size)
            x = x_ref[rows, :]
            g = jnp.dot(x, wgb_ref[...], preferred_element_type=F32)
            u = jnp.dot(x, wub_ref[...], preferred_element_type=F32)
            o_ref[rows, :] = (_silu(g) * u).astype(o_ref.dtype)

        _for_each_row_block(n, ROW_SIZES_CHUNK, blk)

        def zero(r, carry):
            rows = pl.ds(pl.multiple_of(r * MOE_ALIGN, MOE_ALIGN), MOE_ALIGN)
            o_ref[rows, :] = jnp.zeros((MOE_ALIGN, o_ref.shape[1]), o_ref.dtype)
            return carry

        lax.fori_loop(n // MOE_ALIGN, o_ref.shape[0] // MOE_ALIGN, zero, 0)


def _moe_gmm1(xs, w_gu, plan):
    rows, d = xs.shape
    f = w_gu.shape[2] // 2
    tn = _pick_tile(f, MOE_COLS, V7X_LANES)
    nj = f // tn
    w_max = plan["w_max"]

    def col(w, j, cn):
        return jnp.where(cn[w] > 0, j, nj - 1)

    grid_spec = pltpu.PrefetchScalarGridSpec(
        num_scalar_prefetch=3,
        grid=(w_max, nj),
        in_specs=[
            pl.BlockSpec((MOE_CHUNK, d), lambda w, j, ce, cb, cn: (cb[w], 0)),
            pl.BlockSpec((None, d, tn), lambda w, j, ce, cb, cn: (ce[w], 0, col(w, j, cn))),
            pl.BlockSpec((None, d, tn), lambda w, j, ce, cb, cn: (ce[w], 0, col(w, j, cn) + nj)),
        ],
        out_specs=pl.BlockSpec((MOE_CHUNK, tn), lambda w, j, ce, cb, cn: (cb[w], col(w, j, cn))),
        scratch_shapes=[pltpu.VMEM((d, tn), BF16), pltpu.VMEM((d, tn), BF16)],
    )
    return pl.pallas_call(
        _gmm1_kernel,
        out_shape=jax.ShapeDtypeStruct((rows, f), BF16),
        grid_spec=grid_spec,
        compiler_params=_params(("arbitrary", "arbitrary")),
        name="moe_gmm1",
    )(plan["chunk_e"], plan["chunk_blk"], plan["chunk_n"], xs, w_gu, w_gu)


def _gmm2_kernel(re_ref, rb_ref, rn_ref, h_ref, w_ref, o_ref, wb_ref):
    n = rn_ref[pl.program_id(0)]
    kk = pl.program_id(1)

    @pl.when(n > 0)
    def _():
        @pl.when(kk == 0)
        def _():
            o_ref[...] = jnp.zeros_like(o_ref)

        wb_ref[...] = w_ref[...].astype(BF16)

        def blk(off, size):
            rows = pl.ds(off, size)
            o_ref[rows, :] += jnp.dot(h_ref[rows, :], wb_ref[...],
                                      preferred_element_type=F32)

        _for_each_row_block(n, ROW_SIZES_ROWBLK, blk)


def _moe_gmm2(hs, w_down, plan):
    rows, f = hs.shape
    d = w_down.shape[2]
    tk = _pick_tile(f, MOE_K, V7X_LANES)
    nk = f // tk
    r_max = plan["rb_blk"].shape[0]

    def kblk(r, kk, rn):
        return jnp.where(rn[r] > 0, kk, nk - 1)

    grid_spec = pltpu.PrefetchScalarGridSpec(
        num_scalar_prefetch=3,
        grid=(r_max, nk),
        in_specs=[
            pl.BlockSpec((MOE_ROWBLK, tk), lambda r, kk, re, rb, rn: (rb[r], kblk(r, kk, rn))),
            pl.BlockSpec((None, tk, d), lambda r, kk, re, rb, rn: (re[r], kblk(r, kk, rn), 0)),
        ],
        out_specs=pl.BlockSpec((MOE_ROWBLK, d), lambda r, kk, re, rb, rn: (rb[r], 0)),
        scratch_shapes=[pltpu.VMEM((tk, d), BF16)],
    )
    return pl.pallas_call(
        _gmm2_kernel,
        out_shape=jax.ShapeDtypeStruct((rows, d), F32),
        grid_spec=grid_spec,
        compiler_params=_params(("arbitrary", "arbitrary")),
        name="moe_gmm2",
    )(plan["rb_e"], plan["rb_blk"], plan["rb_n"], hs, w_down)


def _combine_kernel(slot_ref, x_ref, route_ref, ys_hbm, g_ref, b_ref, y_ref, yb_ref,
                    buf_ref, sem, *, alpha):
    n_rows = x_ref.shape[0]

    def issue(r, carry):
        _row_copy(ys_hbm, slot_ref[0, 0, 2 * r], buf_ref.at[0], r, sem).start()
        _row_copy(ys_hbm, slot_ref[0, 0, 2 * r + 1], buf_ref.at[1], r, sem).start()
        return carry

    def wait(r, carry):
        _row_copy(ys_hbm, 0, buf_ref.at[0], r, sem).wait()
        _row_copy(ys_hbm, 0, buf_ref.at[1], r, sem).wait()
        return carry

    lax.fori_loop(0, n_rows, issue, 0)
    lax.fori_loop(0, n_rows, wait, 0)
    w1 = route_ref[:, 2:3]
    w2 = route_ref[:, 3:4]
    z = alpha * x_ref[...] + (w1 * buf_ref[0] + w2 * buf_ref[1])
    y = _layer_norm_rows(z, g_ref[...], b_ref[...])
    y_ref[...] = y
    yb_ref[...] = y.astype(BF16)


def _moe_combine(x, route, ys, plan, ln_g, ln_b, alpha):
    n, d = x.shape
    tq = _pick_tile(n, COMBINE_ROWS, V7X_BF16_ROWS)
    nt = n // tq
    slots = plan["slot"].reshape(nt, 1, 2 * tq)
    return pl.pallas_call(
        functools.partial(_combine_kernel, alpha=alpha),
        out_shape=[jax.ShapeDtypeStruct((n, d), F32), jax.ShapeDtypeStruct((n, d), BF16)],
        grid=(nt,),
        in_specs=[
            pl.BlockSpec((1, 1, 2 * tq), lambda i: (i, 0, 0), memory_space=pltpu.SMEM),
            pl.BlockSpec((tq, d), lambda i: (i, 0)),
            pl.BlockSpec((tq, V7X_LANES), lambda i: (i, 0)),
            pl.BlockSpec(memory_space=pl.ANY),
            pl.BlockSpec((1, d), lambda i: (0, 0)),
            pl.BlockSpec((1, d), lambda i: (0, 0)),
        ],
        out_specs=[pl.BlockSpec((tq, d), lambda i: (i, 0)),
                   pl.BlockSpec((tq, d), lambda i: (i, 0))],
        scratch_shapes=[pltpu.VMEM((2, tq, d), F32), pltpu.SemaphoreType.DMA(())],
        compiler_params=_params(("arbitrary",)),
        name="moe_combine",
    )(slots, x, route, ys, ln_g.reshape(1, d), ln_b.reshape(1, d))


def _moe_ffn(x, route, w_gu, w_down, ln_g, ln_b, alpha):
    n = x.shape[0]
    i1 = route[:, 0].astype(jnp.int32)
    i2 = route[:, 1].astype(jnp.int32)
    plan = _route_plan(i1, i2, n)
    xs = _moe_gather(x, plan)
    hs = _moe_gmm1(xs, w_gu, plan)
    ys = _moe_gmm2(hs, w_down, plan)
    return _moe_combine(x, route, ys, plan, ln_g, ln_b, alpha)


def kernel(x_prompt, x_sample, state_conv, cache_k, cache_v, ln1_g, ln1_b, ln2_g, ln2_b,
           w_pw1, b_pw1, w_dw, b_dw, conv_norm_g, conv_norm_b, w_pw2, b_pw2,
           w_qkv, attn_sinks, w_o, w_ffn_gu, w_ffn_down, w_router, w_exp_gu, w_exp_down):
    batch, seq, d_model = x_prompt.shape
    dec_batch, dec_seq, _ = x_sample.shape
    depth = ln1_g.shape[0]
    alpha = float((2 * depth) ** 0.25)
    hd = d_model // N_HEADS
    kvw = N_KV_HEADS * hd
    n_p = batch * seq
    assert seq >= CONV_HIST and seq % WINDOW == 0

    x = jnp.concatenate([x_prompt.reshape(n_p, d_model),
                         x_sample.reshape(dec_batch * dec_seq, d_model)], axis=0)
    xb = x.astype(BF16)
    conv_p, conv_s, k_p, v_p, k_s, v_s = [], [], [], [], [], []

    for i in range(depth):
        j = i // 2
        if i % 2 == 0:
            u = _mm1_pair(xb, w_pw1[j], b_pw1[j], "glu", F32, f"pw1_glu_{i}")
            c_dim = u.shape[1]
            u_p = u[:n_p].reshape(batch, seq, c_dim)
            u_s = u[n_p:].reshape(dec_batch, dec_seq, c_dim)
            ext_s = jnp.concatenate([state_conv[j].astype(F32), u_s], axis=1)
            conv_p.append(u_p[:, -CONV_HIST:])
            conv_s.append(ext_s[:, -CONV_HIST:])
            c = jnp.concatenate([
                _conv_prompt(u, batch, seq, w_dw[j], b_dw[j],
                             conv_norm_g[j], conv_norm_b[j]),
                _conv_sample(ext_s, w_dw[j], b_dw[j], conv_norm_g[j], conv_norm_b[j]),
            ], axis=0)
            x1, x1b = _mm2(c, w_pw2[j], b_pw2[j], x, ln1_g[i], ln1_b[i], alpha, None,
                           f"pw2_ln_{i}")
            hmid = _mm1_pair(x1b, w_ffn_gu[j], None, "swiglu", BF16, f"ffn_gu_{i}")
            x, xb = _mm2(hmid, w_ffn_down[j], None, x1, ln2_g[i], ln2_b[i], alpha, None,
                         f"ffn_down_ln_{i}")
        else:
            qkv = _mm1_plain(xb, w_qkv[j], F32, f"qkv_{i}")
            k_new = qkv[:, d_model:d_model + kvw]
            v_new = qkv[:, d_model + kvw:]
            kp = k_new[:n_p].reshape(batch, seq, N_KV_HEADS, hd)
            vp = v_new[:n_p].reshape(batch, seq, N_KV_HEADS, hd)
            ks = k_new[n_p:].reshape(dec_batch, dec_seq, N_KV_HEADS, hd)
            vs = v_new[n_p:].reshape(dec_batch, dec_seq, N_KV_HEADS, hd)
            w_buf = cache_k.shape[2]
            k_p.append(kp[:, -WINDOW:])
            v_p.append(vp[:, -WINDOW:])
            k_s.append(jnp.concatenate([cache_k[j].astype(F32), ks], axis=1)[:, -w_buf:])
            v_s.append(jnp.concatenate([cache_v[j].astype(F32), vs], axis=1)[:, -w_buf:])
            q_s = qkv[n_p:, :d_model].reshape(dec_batch, dec_seq, d_model)
            o = jnp.concatenate([
                _attn_prompt(qkv, attn_sinks[j], batch, seq, d_model),
                _attn_sample(q_s, ks, vs, cache_k[j].astype(F32), cache_v[j].astype(F32),
                             attn_sinks[j]),
            ], axis=0)
            x1, x1b, route = _mm2(o, w_o[j], None, x, ln1_g[i], ln1_b[i], alpha,
                                  w_router[j], f"wo_ln_router_{i}")
            x, xb = _moe_ffn(x1, route, w_exp_gu[j], w_exp_down[j], ln2_g[i], ln2_b[i], alpha)

    y_p = x[:n_p].reshape(batch, seq, d_model)
    y_s = x[n_p:].reshape(dec_batch, dec_seq, d_model)
    return (y_p, y_s, jnp.stack(conv_p), jnp.stack(conv_s),
            jnp.stack(k_p), jnp.stack(v_p), jnp.stack(k_s), jnp.stack(v_s))
```

```python
import functools

import jax
import jax.numpy as jnp
from jax import lax
from jax.experimental import pallas as pl
from jax.experimental.pallas import tpu as pltpu

F32 = jnp.float32
BF16 = jnp.bfloat16

N_HEADS = 32
N_KV_HEADS = 4
GROUP = N_HEADS // N_KV_HEADS
WINDOW = 128
CONV_WIDTH = 31
CONV_HIST = CONV_WIDTH - 1
N_EXPERTS = 8
LN_EPS = 1e-5

V7X_LANES = 128
V7X_SUBLANES = 8
V7X_BF16_ROWS = 16
V7X_VMEM_LIMIT = 58 * 1024 * 1024

MM_ROWS_CAP = 1056
MM1_COLS = 512
MM2_K = 512
LN_ROWS_CAP = 264
MOE_CHUNK = 2560
MOE_ROWBLK = 2560
MOE_SUB = 256
MOE_ALIGN = 128
MOE_COLS = 512
MOE_K = 512
ROW_SIZES_CHUNK = (1024, 1024, 512, 256, 128)
ROW_SIZES_ROWBLK = ROW_SIZES_CHUNK
CONV_ROWS = 256
CONV_HALO = 32
CONV_COLS = 256
CONV_ROW_CHUNK = 32
COMBINE_ROWS = 256
NEG_INF = float("-inf")


def _pick_tile(n, cap, mult):
    best = None
    for d in range(mult, min(n, cap) + 1, mult):
        if n % d == 0:
            best = d
    if best is None:
        raise ValueError(f"no tile for {n} (cap {cap}, multiple of {mult})")
    return best


def _params(semantics):
    return pltpu.CompilerParams(dimension_semantics=semantics,
                                vmem_limit_bytes=V7X_VMEM_LIMIT)


def _silu(x):
    return x * jax.nn.sigmoid(x)


def _layer_norm_rows(z, g, b):
    mu = jnp.mean(z, axis=-1, keepdims=True)
    d = z - mu
    var = jnp.mean(d * d, axis=-1, keepdims=True)
    return d * lax.rsqrt(var + LN_EPS) * g + b


def _mm1_pair_kernel(*refs, mode, has_bias):
    if has_bias:
        x_ref, wa_ref, wb_ref, ba_ref, bb_ref, o_ref = refs
    else:
        x_ref, wa_ref, wb_ref, o_ref = refs
    x = x_ref[...]
    a = jnp.dot(x, wa_ref[...].astype(BF16), preferred_element_type=F32)
    b = jnp.dot(x, wb_ref[...].astype(BF16), preferred_element_type=F32)
    if has_bias:
        a = a + ba_ref[...]
        b = b + bb_ref[...]
    if mode == "glu":
        r = a * jax.nn.sigmoid(b)
    else:
        r = _silu(a) * b
    o_ref[...] = r.astype(o_ref.dtype)


def _row_vec(stack):
    return stack.reshape(stack.shape[0], 1, stack.shape[1])


def _mm1_pair(x, w, layer, bias, mode, out_dtype, name):
    m, k = x.shape
    h = w.shape[2] // 2
    tm = _pick_tile(m, MM_ROWS_CAP, V7X_BF16_ROWS)
    tn = _pick_tile(h, MM1_COLS, V7X_LANES)
    nj = h // tn
    in_specs = [
        pl.BlockSpec((tm, k), lambda i, j: (i, 0)),
        pl.BlockSpec((None, k, tn), lambda i, j: (layer, 0, j)),
        pl.BlockSpec((None, k, tn), lambda i, j: (layer, 0, j + nj)),
    ]
    args = [x, w, w]
    if bias is not None:
        b2 = _row_vec(bias)
        in_specs += [pl.BlockSpec((None, 1, tn), lambda i, j: (layer, 0, j)),
                     pl.BlockSpec((None, 1, tn), lambda i, j: (layer, 0, j + nj))]
        args += [b2, b2]
    return pl.pallas_call(
        functools.partial(_mm1_pair_kernel, mode=mode, has_bias=bias is not None),
        out_shape=jax.ShapeDtypeStruct((m, h), out_dtype),
        grid=(m // tm, nj),
        in_specs=in_specs,
        out_specs=pl.BlockSpec((tm, tn), lambda i, j: (i, j)),
        compiler_params=_params(("parallel", "arbitrary")),
        name=name,
    )(*args)


def _mm1_plain_kernel(x_ref, w_ref, o_ref):
    o_ref[...] = jnp.dot(x_ref[...], w_ref[...].astype(BF16),
                         preferred_element_type=F32).astype(o_ref.dtype)


def _mm1_plain(x, w, layer, out_dtype, name):
    m, k = x.shape
    n = w.shape[2]
    tm = _pick_tile(m, MM_ROWS_CAP, V7X_BF16_ROWS)
    tn = _pick_tile(n, MM1_COLS, V7X_LANES)
    return pl.pallas_call(
        _mm1_plain_kernel,
        out_shape=jax.ShapeDtypeStruct((m, n), out_dtype),
        grid=(m // tm, n // tn),
        in_specs=[pl.BlockSpec((tm, k), lambda i, j: (i, 0)),
                  pl.BlockSpec((None, k, tn), lambda i, j: (layer, 0, j))],
        out_specs=pl.BlockSpec((tm, tn), lambda i, j: (i, j)),
        compiler_params=_params(("parallel", "arbitrary")),
        name=name,
    )(x, w)


def _route_from_logits(logits):
    lane = lax.broadcasted_iota(jnp.int32, logits.shape, 1).astype(F32)
    no_lane = float(V7X_LANES)
    lg = jnp.where(lane < N_EXPERTS, logits, NEG_INF)
    m1 = jnp.max(lg, axis=-1, keepdims=True)
    i1 = jnp.min(jnp.where(lg == m1, lane, no_lane), axis=-1, keepdims=True)
    lg2 = jnp.where(lane == i1, NEG_INF, lg)
    m2 = jnp.max(lg2, axis=-1, keepdims=True)
    i2 = jnp.min(jnp.where(lg2 == m2, lane, no_lane), axis=-1, keepdims=True)
    e2 = jnp.exp(m2 - m1)
    den = 1.0 + e2
    w1 = 1.0 / den
    w2 = e2 / den
    out = jnp.where(lane == 0.0, i1,
                    jnp.where(lane == 1.0, i2,
                              jnp.where(lane == 2.0, w1,
                                        jnp.where(lane == 3.0, w2, 0.0))))
    return out


def _split_bf16(v):
    hi = v.astype(BF16)
    lo = (v - hi.astype(F32)).astype(BF16)
    return hi, lo


def _mm2_kernel(*refs, alpha, has_bias, has_router, row_half, ln_rows):
    refs = list(refs)
    h_ref, w_ref, r_ref, g_ref, b_ref = refs[:5]
    pos = 5
    bias_ref = None
    wr_ref = None
    if has_bias:
        bias_ref = refs[pos]
        pos += 1
    if has_router:
        wr_ref = refs[pos]
        pos += 1
    y_ref, yb_ref = refs[pos], refs[pos + 1]
    pos += 2
    route_ref = refs[pos] if has_router else None

    kk = pl.program_id(1)
    tm = y_ref.shape[0]

    @pl.when(kk == 0)
    def _():
        z0 = alpha * r_ref[...]
        if has_bias:
            z0 = z0 + bias_ref[...]
        y_ref[...] = z0

    wb = w_ref[...].astype(BF16)
    for s in range(tm // row_half):
        rows = pl.ds(s * row_half, row_half)
        y_ref[rows, :] += jnp.dot(h_ref[rows, :], wb, preferred_element_type=F32)

    @pl.when(kk == pl.num_programs(1) - 1)
    def _():
        g = g_ref[...]
        b = b_ref[...]
        if has_router:
            wr = wr_ref[...]
            wr_hi, wr_lo = _split_bf16(wr)

        def ln_chunk(c, carry):
            rows = pl.ds(pl.multiple_of(c * ln_rows, V7X_SUBLANES), ln_rows)
            y = _layer_norm_rows(y_ref[rows, :], g, b)
            y_ref[rows, :] = y
            yb_ref[rows, :] = y.astype(BF16)
            if has_router:
                y_hi, y_lo = _split_bf16(y)
                logits = (jnp.dot(y_hi, wr_hi, preferred_element_type=F32)
                          + jnp.dot(y_hi, wr_lo, preferred_element_type=F32)
                          + jnp.dot(y_lo, wr_hi, preferred_element_type=F32))
                route_ref[rows, :] = _route_from_logits(logits)
            return carry

        lax.fori_loop(0, tm // ln_rows, ln_chunk, 0)


def _mm2(h, w, w_layer, bias, resid, ln_g, ln_b, ln_layer, alpha, w_router, name):
    m, k = h.shape
    d = w.shape[2]
    tm = _pick_tile(m, MM_ROWS_CAP, V7X_BF16_ROWS)
    tk = _pick_tile(k, MM2_K, V7X_LANES)
    row_half = tm // 2 if (tm // 2) % V7X_BF16_ROWS == 0 else tm
    ln_rows = _pick_tile(tm, LN_ROWS_CAP, V7X_SUBLANES)
    once = pl.Buffered(1)
    in_specs = [
        pl.BlockSpec((tm, tk), lambda i, kk: (i, kk)),
        pl.BlockSpec((None, tk, d), lambda i, kk: (w_layer, kk, 0)),
        pl.BlockSpec((tm, d), lambda i, kk: (i, 0), pipeline_mode=once),
        pl.BlockSpec((None, 1, d), lambda i, kk: (ln_layer, 0, 0)),
        pl.BlockSpec((None, 1, d), lambda i, kk: (ln_layer, 0, 0)),
    ]
    args = [h, w, resid, _row_vec(ln_g), _row_vec(ln_b)]
    if bias is not None:
        in_specs.append(pl.BlockSpec((None, 1, d), lambda i, kk: (w_layer, 0, 0)))
        args.append(_row_vec(bias))
    out_shape = [jax.ShapeDtypeStruct((m, d), F32), jax.ShapeDtypeStruct((m, d), BF16)]
    out_specs = [pl.BlockSpec((tm, d), lambda i, kk: (i, 0)),
                 pl.BlockSpec((tm, d), lambda i, kk: (i, 0))]
    if w_router is not None:
        wr = jnp.pad(w_router, ((0, 0), (0, 0), (0, V7X_LANES - w_router.shape[2])))
        in_specs.append(pl.BlockSpec((None, d, V7X_LANES), lambda i, kk: (w_layer, 0, 0)))
        args.append(wr)
        out_shape.append(jax.ShapeDtypeStruct((m, V7X_LANES), F32))
        out_specs.append(pl.BlockSpec((tm, V7X_LANES), lambda i, kk: (i, 0)))
    return pl.pallas_call(
        functools.partial(_mm2_kernel, alpha=alpha, has_bias=bias is not None,
                          has_router=w_router is not None, row_half=row_half,
                          ln_rows=ln_rows),
        out_shape=out_shape,
        grid=(m // tm, k // tk),
        in_specs=in_specs,
        out_specs=out_specs,
        compiler_params=_params(("parallel", "arbitrary")),
        name=name,
    )(*args)


def _conv_ln_silu(ext_ref, pre_ref, w_ref, bdw_ref, g_ref, b_ref, o_ref, t_rows):
    c_total = pre_ref.shape[1]
    cols_blk = min(CONV_COLS, c_total)
    row_chunk = min(CONV_ROW_CHUNK, t_rows)
    lead = CONV_HALO - CONV_HIST

    def col_body(c, carry):
        cols = pl.ds(pl.multiple_of(c * cols_blk, V7X_LANES), cols_blk)
        bdw = bdw_ref[:, cols]
        for r in range(t_rows // row_chunk):
            acc = jnp.broadcast_to(bdw, (row_chunk, cols_blk))
            for k in range(CONV_WIDTH):
                acc = acc + w_ref[pl.ds(k, 1), cols] * ext_ref[
                    pl.ds(r * row_chunk + lead + k, row_chunk), cols]
            pre_ref[pl.ds(r * row_chunk, row_chunk), cols] = acc
        return carry

    lax.fori_loop(0, c_total // cols_blk, col_body, 0)
    y = _layer_norm_rows(pre_ref[...], g_ref[...], b_ref[...])
    o_ref[...] = _silu(y).astype(o_ref.dtype)


def _conv_prompt_kernel(prev_ref, cur_ref, w_ref, bdw_ref, g_ref, b_ref, o_ref,
                        ext_ref, pre_ref):
    i = pl.program_id(1)
    t_rows = cur_ref.shape[0]
    prev = prev_ref[...]
    ext_ref[pl.ds(0, CONV_HALO), :] = jnp.where(i == 0, jnp.zeros_like(prev), prev)
    ext_ref[pl.ds(CONV_HALO, t_rows), :] = cur_ref[...]
    _conv_ln_silu(ext_ref, pre_ref, w_ref, bdw_ref, g_ref, b_ref, o_ref, t_rows)


def _conv_param_specs(c, layer):
    def fixed(*_):
        return (layer, 0, 0)
    return [pl.BlockSpec((None, CONV_WIDTH, c), fixed),
            pl.BlockSpec((None, 1, c), fixed),
            pl.BlockSpec((None, 1, c), fixed),
            pl.BlockSpec((None, 1, c), fixed)]


def _conv_prompt(u, batch, seq, layer, w_dw, b_dw, g, b):
    rows, c = u.shape
    t = _pick_tile(seq, CONV_ROWS, CONV_HALO)
    nt = seq // t
    halo_per_t = t // CONV_HALO

    def prev_map(bi, i):
        return (jnp.maximum(bi * (seq // CONV_HALO) + i * halo_per_t - 1, 0), 0)

    return pl.pallas_call(
        _conv_prompt_kernel,
        out_shape=jax.ShapeDtypeStruct((rows, c), BF16),
        grid=(batch, nt),
        in_specs=[
            pl.BlockSpec((CONV_HALO, c), prev_map),
            pl.BlockSpec((t, c), lambda bi, i: (bi * nt + i, 0)),
        ] + _conv_param_specs(c, layer),
        out_specs=pl.BlockSpec((t, c), lambda bi, i: (bi * nt + i, 0)),
        scratch_shapes=[pltpu.VMEM((t + CONV_HALO, c), F32), pltpu.VMEM((t, c), F32)],
        compiler_params=_params(("parallel", "arbitrary")),
        name="conv_prompt",
    )(u, u, w_dw, _row_vec(b_dw), _row_vec(g), _row_vec(b))


def _conv_sample_kernel(ext_ref, w_ref, bdw_ref, g_ref, b_ref, full_ref, o_ref, pre_ref):
    del full_ref
    _conv_ln_silu(ext_ref, pre_ref, w_ref, bdw_ref, g_ref, b_ref, o_ref, o_ref.shape[0])


def _conv_sample(ext, full, row0, layer, w_dw, b_dw, g, b):
    s, rows, c = ext.shape
    t = rows - CONV_HIST
    assert row0 % t == 0
    lead = CONV_HALO - CONV_HIST
    ext = jnp.pad(ext, ((0, 0), (lead, 0), (0, 0)))
    return pl.pallas_call(
        _conv_sample_kernel,
        out_shape=jax.ShapeDtypeStruct(full.shape, full.dtype),
        grid=(s,),
        in_specs=[pl.BlockSpec((None, rows + lead, c), lambda i: (i, 0, 0))]
        + _conv_param_specs(c, layer) + [pl.BlockSpec(memory_space=pl.ANY)],
        out_specs=pl.BlockSpec((t, c), lambda i: (row0 // t + i, 0)),
        scratch_shapes=[pltpu.VMEM((t, c), F32)],
        input_output_aliases={5: 0},
        compiler_params=_params(("parallel",)),
        name="conv_sample",
    )(ext, w_dw, _row_vec(b_dw), _row_vec(g), _row_vec(b), full)


def _attend(q, k, v, mask, sink, scale):
    s = lax.dot_general(q, k, (((1,), (1,)), ((), ())),
                        preferred_element_type=F32) * scale
    s = jnp.where(mask, s, NEG_INF)
    m = jnp.maximum(jnp.max(s, axis=-1, keepdims=True), sink)
    p = jnp.exp(s - m)
    den = jnp.sum(p, axis=-1, keepdims=True) + jnp.exp(sink - m)
    o = jnp.dot(p.astype(BF16), v, preferred_element_type=F32)
    return o / den


def _attn_prompt_kernel(sink_ref, q_ref, kp_ref, kc_ref, vp_ref, vc_ref, o_ref, *, hd):
    n = pl.program_id(1)
    blk = q_ref.shape[0]
    scale = hd ** -0.5
    rows = GROUP * blk
    band = (GROUP, blk, 2 * blk)
    qi = lax.broadcasted_iota(jnp.int32, band, 1).reshape(rows, 2 * blk)
    ci = lax.broadcasted_iota(jnp.int32, band, 2).reshape(rows, 2 * blk)
    gi = lax.broadcasted_iota(jnp.int32, (GROUP, blk, 1), 0).reshape(rows, 1)
    mask = (ci >= qi) & (ci <= qi + WINDOW) & ((ci >= blk) | (n > 0))
    kband = jnp.concatenate([kp_ref[...], kc_ref[...]], axis=0).astype(BF16)
    vband = jnp.concatenate([vp_ref[...], vc_ref[...]], axis=0).astype(BF16)
    for h in range(N_KV_HEADS):
        kh = kband[:, h * hd:(h + 1) * hd]
        vh = vband[:, h * hd:(h + 1) * hd]
        heads = [h * GROUP + g for g in range(GROUP)]
        q = jnp.concatenate([q_ref[:, hh * hd:(hh + 1) * hd] for hh in heads],
                            axis=0).astype(BF16)
        sink = jnp.zeros((rows, 1), F32)
        for g, hh in enumerate(heads):
            sink = jnp.where(gi == g, sink_ref[hh], sink)
        o = _attend(q, kh, vh, mask, sink, scale).astype(o_ref.dtype)
        for g, hh in enumerate(heads):
            o_ref[:, hh * hd:(hh + 1) * hd] = o[g * blk:(g + 1) * blk]


def _attn_prompt(qkv, sinks, batch, seq, d_model):
    hd = d_model // N_HEADS
    kvw = N_KV_HEADS * hd
    blk = WINDOW
    nb = seq // blk
    qcols = d_model // kvw

    def cur(off):
        return lambda bi, n, s: (bi * nb + n, off)

    def prev(off):
        return lambda bi, n, s: (bi * nb + jnp.maximum(n - 1, 0), off)

    grid_spec = pltpu.PrefetchScalarGridSpec(
        num_scalar_prefetch=1,
        grid=(batch, nb),
        in_specs=[
            pl.BlockSpec((blk, d_model), lambda bi, n, s: (bi * nb + n, 0)),
            pl.BlockSpec((blk, kvw), prev(qcols)),
            pl.BlockSpec((blk, kvw), cur(qcols)),
            pl.BlockSpec((blk, kvw), prev(qcols + 1)),
            pl.BlockSpec((blk, kvw), cur(qcols + 1)),
        ],
        out_specs=pl.BlockSpec((blk, d_model), lambda bi, n, s: (bi * nb + n, 0)),
    )
    return pl.pallas_call(
        functools.partial(_attn_prompt_kernel, hd=hd),
        out_shape=jax.ShapeDtypeStruct((qkv.shape[0], d_model), BF16),
        grid_spec=grid_spec,
        compiler_params=_params(("parallel", "arbitrary")),
        name="attn_prompt",
    )(sinks, qkv, qkv, qkv, qkv, qkv)


def _attn_sample_kernel(sink_ref, q_ref, kn_ref, vn_ref, kc_ref, vc_ref, full_ref, o_ref, *, hd):
    del full_ref
    scale = hd ** -0.5
    t_new = q_ref.shape[0]
    w_buf = kc_ref.shape[1]
    rows = GROUP * t_new
    s_len = w_buf + t_new
    grid3 = (GROUP, t_new, s_len)
    ti = lax.broadcasted_iota(jnp.int32, grid3, 1).reshape(rows, s_len)
    ci = lax.broadcasted_iota(jnp.int32, grid3, 2).reshape(rows, s_len)
    gi = lax.broadcasted_iota(jnp.int32, (GROUP, t_new, 1), 0).reshape(rows, 1)
    mask = (ci >= ti + (w_buf - WINDOW)) & (ci <= ti + w_buf)
    for h in range(N_KV_HEADS):
        cols = slice(h * hd, (h + 1) * hd)
        k = jnp.concatenate([kc_ref[h], kn_ref[:, cols]], axis=0).astype(BF16)
        v = jnp.concatenate([vc_ref[h], vn_ref[:, cols]], axis=0).astype(BF16)
        heads = [h * GROUP + g for g in range(GROUP)]
        q = jnp.concatenate([q_ref[:, hh * hd:(hh + 1) * hd] for hh in heads],
                            axis=0).astype(BF16)
        sink = jnp.zeros((rows, 1), F32)
        for g, hh in enumerate(heads):
            sink = jnp.where(gi == g, sink_ref[hh], sink)
        o = _attend(q, k, v, mask, sink, scale).astype(o_ref.dtype)
        for g, hh in enumerate(heads):
            o_ref[:, hh * hd:(hh + 1) * hd] = o[g * t_new:(g + 1) * t_new]


def _attn_sample(qkv, full, row0, n_seq, t_new, k_buf, v_buf, sinks, d_model):
    hd = d_model // N_HEADS
    kvw = N_KV_HEADS * hd
    w_buf = k_buf.shape[1]
    assert row0 % t_new == 0
    r0 = row0 // t_new
    qcols = d_model // kvw
    heads_first = lambda a: a.transpose(0, 2, 1, 3)
    cache = pl.BlockSpec((None, N_KV_HEADS, w_buf, hd), lambda i, sk: (i, 0, 0, 0))
    grid_spec = pltpu.PrefetchScalarGridSpec(
        num_scalar_prefetch=1,
        grid=(n_seq,),
        in_specs=[
            pl.BlockSpec((t_new, d_model), lambda i, sk: (r0 + i, 0)),
            pl.BlockSpec((t_new, kvw), lambda i, sk: (r0 + i, qcols)),
            pl.BlockSpec((t_new, kvw), lambda i, sk: (r0 + i, qcols + 1)),
            cache, cache,
            pl.BlockSpec(memory_space=pl.ANY),
        ],
        out_specs=pl.BlockSpec((t_new, d_model), lambda i, sk: (r0 + i, 0)),
    )
    return pl.pallas_call(
        functools.partial(_attn_sample_kernel, hd=hd),
        out_shape=jax.ShapeDtypeStruct(full.shape, full.dtype),
        grid_spec=grid_spec,
        input_output_aliases={6: 0},
        compiler_params=_params(("parallel",)),
        name="attn_sample",
    )(sinks, qkv, qkv, qkv, heads_first(k_buf), heads_first(v_buf), full)


def _compact(active, size):
    order = jnp.argsort(jnp.logical_not(active).astype(jnp.int32), stable=True)
    order = order[:size].astype(jnp.int32)
    return order, jnp.sum(active).astype(jnp.int32)


def _route_plan(i1, i2, n_tok):
    n_asg = 2 * n_tok
    halves = MOE_CHUNK // MOE_ROWBLK
    subs = MOE_CHUNK // MOE_SUB
    w_max = n_asg // MOE_CHUNK + N_EXPERTS
    r_max = n_asg // MOE_ROWBLK + N_EXPERTS
    s_max = n_asg // MOE_SUB + N_EXPERTS

    ei = jnp.stack([i1, i2], axis=1).reshape(-1)
    onehot = (ei[:, None] == jnp.arange(N_EXPERTS, dtype=jnp.int32)[None, :]).astype(jnp.int32)
    csum = jnp.cumsum(onehot, axis=0)
    counts = csum[-1]
    rank = jnp.take_along_axis(csum, ei[:, None], axis=1)[:, 0] - 1
    nch = (counts + MOE_CHUNK - 1) // MOE_CHUNK
    cend = jnp.cumsum(nch)
    cstart = cend - nch
    slot = (cstart[ei] * MOE_CHUNK + rank).astype(jnp.int32)
    n_chunks = cend[-1]

    wid = jnp.arange(w_max, dtype=jnp.int32)
    wcl = jnp.minimum(wid, n_chunks - 1)
    ce = jnp.sum((cend[None, :] <= wcl[:, None]).astype(jnp.int32), axis=1)
    ce = jnp.minimum(ce, N_EXPERTS - 1).astype(jnp.int32)
    crow = jnp.clip(counts[ce] - (wcl - cstart[ce]) * MOE_CHUNK, 0, MOE_CHUNK)
    crow = jnp.where(wid < n_chunks, crow, 0).astype(jnp.int32)

    def round_up(v, mlt):
        return ((v + mlt - 1) // mlt) * mlt

    rb_rows = jnp.clip(crow[:, None] - jnp.arange(halves, dtype=jnp.int32)[None, :] * MOE_ROWBLK,
                       0, MOE_ROWBLK).reshape(-1)
    rb_ids, n_rb = _compact(rb_rows > 0, r_max)
    rb_ids = rb_ids[jnp.minimum(jnp.arange(r_max), n_rb - 1)]
    rb_n = jnp.where(jnp.arange(r_max) < n_rb, round_up(rb_rows[rb_ids], MOE_ALIGN), 0)
    rb_e = ce[rb_ids // halves]

    sb_rows = jnp.clip(crow[:, None] - jnp.arange(subs, dtype=jnp.int32)[None, :] * MOE_SUB,
                       0, MOE_SUB).reshape(-1)
    sb_ids, n_sb = _compact(sb_rows > 0, s_max)
    sb_ids = sb_ids[jnp.minimum(jnp.arange(s_max), n_sb - 1)]
    src = jnp.zeros((w_max * MOE_CHUNK,), jnp.int32).at[slot].set(
        jnp.arange(n_asg, dtype=jnp.int32) // 2)
    src_sb = src.reshape(w_max * subs, MOE_SUB)[sb_ids]

    return dict(
        slot=slot, w_max=w_max,
        chunk_e=ce, chunk_blk=wcl.astype(jnp.int32),
        chunk_n=round_up(crow, MOE_ALIGN).astype(jnp.int32),
        rb_e=rb_e.astype(jnp.int32), rb_blk=rb_ids.astype(jnp.int32),
        rb_n=rb_n.astype(jnp.int32),
        sb_blk=sb_ids.astype(jnp.int32), n_sb=n_sb.reshape(1),
        src_sb=src_sb.reshape(s_max, 1, MOE_SUB).astype(jnp.int32),
    )


def _row_copy(src_hbm, row, dst_ref, dst_row, sem):
    return pltpu.make_async_copy(src_hbm.at[pl.ds(row, 1)], dst_ref.at[pl.ds(dst_row, 1)], sem)


def _gather_kernel(sb_ref, nsb_ref, src_ref, x_hbm, o_ref, buf_ref, sem):
    i = pl.program_id(0)
    n_rows = buf_ref.shape[0]

    @pl.when(i < nsb_ref[0])
    def _():
        def issue(r, carry):
            _row_copy(x_hbm, src_ref[0, 0, r], buf_ref, r, sem).start()
            return carry

        def wait(r, carry):
            _row_copy(x_hbm, 0, buf_ref, r, sem).wait()
            return carry

        lax.fori_loop(0, n_rows, issue, 0)
        lax.fori_loop(0, n_rows, wait, 0)
        o_ref[...] = buf_ref[...].astype(o_ref.dtype)


def _moe_gather(x, plan):
    d = x.shape[1]
    s_max = plan["sb_blk"].shape[0]
    grid_spec = pltpu.PrefetchScalarGridSpec(
        num_scalar_prefetch=2,
        grid=(s_max,),
        in_specs=[
            pl.BlockSpec((1, 1, MOE_SUB), lambda i, sb, ns: (i, 0, 0),
                         memory_space=pltpu.SMEM),
            pl.BlockSpec(memory_space=pl.ANY),
        ],
        out_specs=pl.BlockSpec((MOE_SUB, d), lambda i, sb, ns: (sb[i], 0)),
        scratch_shapes=[pltpu.VMEM((MOE_SUB, d), F32), pltpu.SemaphoreType.DMA(())],
    )
    return pl.pallas_call(
        _gather_kernel,
        out_shape=jax.ShapeDtypeStruct((plan["w_max"] * MOE_CHUNK, d), BF16),
        grid_spec=grid_spec,
        compiler_params=_params(("arbitrary",)),
        name="moe_gather",
    )(plan["sb_blk"], plan["n_sb"], plan["src_sb"], x)


def _for_each_row_block(n_rows, sizes, fn):
    off = jnp.int32(0)
    for size in sizes:
        take = (n_rows - off) >= size

        @pl.when(take)
        def _(off=off, size=size):
            fn(pl.multiple_of(off, MOE_ALIGN), size)

        off = off + jnp.where(take, size, 0)


def _gmm1_kernel(ce_ref, cb_ref, cn_ref, x_ref, wg_ref, wu_ref, o_ref, wgb_ref, wub_ref):
    n = cn_ref[pl.program_id(0)]

    @pl.when(n > 0)
    def _():
        wgb_ref[...] = wg_ref[...].astype(BF16)
        wub_ref[...] = wu_ref[...].astype(BF16)

        def blk(off, size):
            rows = pl.ds(off, size)
            x = x_ref[rows, :]
            g = jnp.dot(x, wgb_ref[...], preferred_element_type=F32)
            u = jnp.dot(x, wub_ref[...], preferred_element_type=F32)
            o_ref[rows, :] = (_silu(g) * u).astype(o_ref.dtype)

        _for_each_row_block(n, ROW_SIZES_CHUNK, blk)

        def zero(r, carry):
            rows = pl.ds(pl.multiple_of(r * MOE_ALIGN, MOE_ALIGN), MOE_ALIGN)
            o_ref[rows, :] = jnp.zeros((MOE_ALIGN, o_ref.shape[1]), o_ref.dtype)
            return carry

        lax.fori_loop(n // MOE_ALIGN, o_ref.shape[0] // MOE_ALIGN, zero, 0)


def _moe_gmm1(xs, w_gu, layer, plan):
    rows, d = xs.shape
    f = w_gu.shape[3] // 2
    tn = _pick_tile(f, MOE_COLS, V7X_LANES)
    nj = f // tn
    w_max = plan["w_max"]

    def col(w, j, cn):
        return jnp.where(cn[w] > 0, j, nj - 1)

    grid_spec = pltpu.PrefetchScalarGridSpec(
        num_scalar_prefetch=3,
        grid=(w_max, nj),
        in_specs=[
            pl.BlockSpec((MOE_CHUNK, d), lambda w, j, ce, cb, cn: (cb[w], 0)),
            pl.BlockSpec((None, None, d, tn),
                         lambda w, j, ce, cb, cn: (layer, ce[w], 0, col(w, j, cn))),
            pl.BlockSpec((None, None, d, tn),
                         lambda w, j, ce, cb, cn: (layer, ce[w], 0, col(w, j, cn) + nj)),
        ],
        out_specs=pl.BlockSpec((MOE_CHUNK, tn), lambda w, j, ce, cb, cn: (cb[w], col(w, j, cn))),
        scratch_shapes=[pltpu.VMEM((d, tn), BF16), pltpu.VMEM((d, tn), BF16)],
    )
    return pl.pallas_call(
        _gmm1_kernel,
        out_shape=jax.ShapeDtypeStruct((rows, f), BF16),
        grid_spec=grid_spec,
        compiler_params=_params(("arbitrary", "arbitrary")),
        name="moe_gmm1",
    )(plan["chunk_e"], plan["chunk_blk"], plan["chunk_n"], xs, w_gu, w_gu)


def _gmm2_kernel(re_ref, rb_ref, rn_ref, h_ref, w_ref, o_ref, wb_ref):
    n = rn_ref[pl.program_id(0)]
    kk = pl.program_id(1)

    @pl.when(n > 0)
    def _():
        @pl.when(kk == 0)
        def _():
            o_ref[...] = jnp.zeros_like(o_ref)

        wb_ref[...] = w_ref[...].astype(BF16)

        def blk(off, size):
            rows = pl.ds(off, size)
            o_ref[rows, :] += jnp.dot(h_ref[rows, :], wb_ref[...],
                                      preferred_element_type=F32)

        _for_each_row_block(n, ROW_SIZES_ROWBLK, blk)


def _moe_gmm2(hs, w_down, layer, plan):
    rows, f = hs.shape
    d = w_down.shape[3]
    tk = _pick_tile(f, MOE_K, V7X_LANES)
    nk = f // tk
    r_max = plan["rb_blk"].shape[0]

    def kblk(r, kk, rn):
        return jnp.where(rn[r] > 0, kk, nk - 1)

    grid_spec = pltpu.PrefetchScalarGridSpec(
        num_scalar_prefetch=3,
        grid=(r_max, nk),
        in_specs=[
            pl.BlockSpec((MOE_ROWBLK, tk), lambda r, kk, re, rb, rn: (rb[r], kblk(r, kk, rn))),
            pl.BlockSpec((None, None, tk, d),
                         lambda r, kk, re, rb, rn: (layer, re[r], kblk(r, kk, rn), 0)),
        ],
        out_specs=pl.BlockSpec((MOE_ROWBLK, d), lambda r, kk, re, rb, rn: (rb[r], 0),
                               pipeline_mode=pl.Buffered(1)),
        scratch_shapes=[pltpu.VMEM((tk, d), BF16)],
    )
    return pl.pallas_call(
        _gmm2_kernel,
        out_shape=jax.ShapeDtypeStruct((rows, d), F32),
        grid_spec=grid_spec,
        compiler_params=_params(("arbitrary", "arbitrary")),
        name="moe_gmm2",
    )(plan["rb_e"], plan["rb_blk"], plan["rb_n"], hs, w_down)


def _combine_kernel(slot_ref, x_ref, route_ref, ys_hbm, g_ref, b_ref, y_ref, yb_ref,
                    buf_ref, sem, *, alpha):
    n_rows = x_ref.shape[0]

    def issue(r, carry):
        _row_copy(ys_hbm, slot_ref[0, 0, 2 * r], buf_ref.at[0], r, sem).start()
        _row_copy(ys_hbm, slot_ref[0, 0, 2 * r + 1], buf_ref.at[1], r, sem).start()
        return carry

    def wait(r, carry):
        _row_copy(ys_hbm, 0, buf_ref.at[0], r, sem).wait()
        _row_copy(ys_hbm, 0, buf_ref.at[1], r, sem).wait()
        return carry

    lax.fori_loop(0, n_rows, issue, 0)
    lax.fori_loop(0, n_rows, wait, 0)
    w1 = route_ref[:, 2:3]
    w2 = route_ref[:, 3:4]
    z = alpha * x_ref[...] + (w1 * buf_ref[0] + w2 * buf_ref[1])
    y = _layer_norm_rows(z, g_ref[...], b_ref[...])
    y_ref[...] = y
    yb_ref[...] = y.astype(BF16)


def _moe_combine(x, route, ys, plan, ln_g, ln_b, ln_layer, alpha):
    n, d = x.shape
    tq = _pick_tile(n, COMBINE_ROWS, V7X_BF16_ROWS)
    nt = n // tq
    slots = plan["slot"].reshape(nt, 1, 2 * tq)
    return pl.pallas_call(
        functools.partial(_combine_kernel, alpha=alpha),
        out_shape=[jax.ShapeDtypeStruct((n, d), F32), jax.ShapeDtypeStruct((n, d), BF16)],
        grid=(nt,),
        in_specs=[
            pl.BlockSpec((1, 1, 2 * tq), lambda i: (i, 0, 0), memory_space=pltpu.SMEM),
            pl.BlockSpec((tq, d), lambda i: (i, 0)),
            pl.BlockSpec((tq, V7X_LANES), lambda i: (i, 0)),
            pl.BlockSpec(memory_space=pl.ANY),
            pl.BlockSpec((None, 1, d), lambda i: (ln_layer, 0, 0)),
            pl.BlockSpec((None, 1, d), lambda i: (ln_layer, 0, 0)),
        ],
        out_specs=[pl.BlockSpec((tq, d), lambda i: (i, 0)),
                   pl.BlockSpec((tq, d), lambda i: (i, 0))],
        scratch_shapes=[pltpu.VMEM((2, tq, d), F32), pltpu.SemaphoreType.DMA(())],
        compiler_params=_params(("arbitrary",)),
        name="moe_combine",
    )(slots, x, route, ys, _row_vec(ln_g), _row_vec(ln_b))


def _moe_ffn(x, route, w_gu, w_down, layer, ln_g, ln_b, ln_layer, alpha):
    n = x.shape[0]
    ids = route[:, :2].astype(jnp.int32)
    plan = _route_plan(ids[:, 0], ids[:, 1], n)
    xs = _moe_gather(x, plan)
    hs = _moe_gmm1(xs, w_gu, layer, plan)
    ys = _moe_gmm2(hs, w_down, layer, plan)
    return _moe_combine(x, route, ys, plan, ln_g, ln_b, ln_layer, alpha)


def kernel(x_prompt, x_sample, state_conv, cache_k, cache_v, ln1_g, ln1_b, ln2_g, ln2_b,
           w_pw1, b_pw1, w_dw, b_dw, conv_norm_g, conv_norm_b, w_pw2, b_pw2,
           w_qkv, attn_sinks, w_o, w_ffn_gu, w_ffn_down, w_router, w_exp_gu, w_exp_down):
    batch, seq, d_model = x_prompt.shape
    dec_batch, dec_seq, _ = x_sample.shape
    depth = ln1_g.shape[0]
    alpha = float((2 * depth) ** 0.25)
    hd = d_model // N_HEADS
    kvw = N_KV_HEADS * hd
    n_p = batch * seq
    assert seq >= CONV_HIST and seq % WINDOW == 0

    x = jnp.concatenate([x_prompt.reshape(n_p, d_model),
                         x_sample.reshape(dec_batch * dec_seq, d_model)], axis=0)
    xb = x.astype(BF16)
    conv_p, conv_s, k_p, v_p, k_s, v_s = [], [], [], [], [], []

    def tail_rows(a, n_rows):
        return jnp.stack([a[(bi + 1) * seq - n_rows:(bi + 1) * seq] for bi in range(batch)])

    for i in range(depth):
        j = i // 2
        if i % 2 == 0:
            u = _mm1_pair(xb, w_pw1, j, b_pw1, "glu", F32, f"pw1_glu_{i}")
            u_s = u[n_p:].reshape(dec_batch, dec_seq, u.shape[1])
            ext_s = jnp.concatenate([state_conv[j].astype(F32), u_s], axis=1)
            conv_p.append(tail_rows(u, CONV_HIST))
            conv_s.append(ext_s[:, -CONV_HIST:])
            c = _conv_prompt(u, batch, seq, j, w_dw, b_dw, conv_norm_g, conv_norm_b)
            c = _conv_sample(ext_s, c, n_p, j, w_dw, b_dw, conv_norm_g, conv_norm_b)
            x1, x1b = _mm2(c, w_pw2, j, b_pw2, x, ln1_g, ln1_b, i, alpha, None, f"pw2_ln_{i}")
            hmid = _mm1_pair(x1b, w_ffn_gu, j, None, "swiglu", BF16, f"ffn_gu_{i}")
            x, xb = _mm2(hmid, w_ffn_down, j, None, x1, ln2_g, ln2_b, i, alpha, None,
                         f"ffn_down_ln_{i}")
        else:
            qkv = _mm1_plain(xb, w_qkv, j, F32, f"qkv_{i}")
            kv_p = tail_rows(qkv[:, d_model:], WINDOW)
            kv_s = qkv[n_p:, d_model:].reshape(dec_batch, dec_seq, 2 * kvw)
            heads = lambda a: a.reshape(a.shape[0], a.shape[1], N_KV_HEADS, hd)
            w_buf = cache_k.shape[2]
            k_buf = cache_k[j].astype(F32)
            v_buf = cache_v[j].astype(F32)
            k_p.append(heads(kv_p[..., :kvw]))
            v_p.append(heads(kv_p[..., kvw:]))
            k_s.append(jnp.concatenate([k_buf, heads(kv_s[..., :kvw])], axis=1)[:, -w_buf:])
            v_s.append(jnp.concatenate([v_buf, heads(kv_s[..., kvw:])], axis=1)[:, -w_buf:])
            o = _attn_prompt(qkv, attn_sinks[j], batch, seq, d_model)
            o = _attn_sample(qkv, o, n_p, dec_batch, dec_seq, k_buf, v_buf, attn_sinks[j],
                             d_model)
            x1, x1b, route = _mm2(o, w_o, j, None, x, ln1_g, ln1_b, i, alpha, w_router,
                                  f"wo_ln_router_{i}")
            x, xb = _moe_ffn(x1, route, w_exp_gu, w_exp_down, j, ln2_g, ln2_b, i, alpha)

    y_p = x[:n_p].reshape(batch, seq, d_model)
    y_s = x[n_p:].reshape(dec_batch, dec_seq, d_model)
    return (y_p, y_s, jnp.stack(conv_p), jnp.stack(conv_s),
            jnp.stack(k_p), jnp.stack(v_p), jnp.stack(k_s), jnp.stack(v_s))
```

```python
import functools

import jax
import jax.numpy as jnp
from jax import lax
from jax.experimental import pallas as pl
from jax.experimental.pallas import tpu as pltpu

F32 = jnp.float32
BF16 = jnp.bfloat16

N_HEADS = 32
N_KV_HEADS = 4
GROUP = N_HEADS // N_KV_HEADS
WINDOW = 128
CONV_WIDTH = 31
CONV_HIST = CONV_WIDTH - 1
N_EXPERTS = 8
LN_EPS = 1e-5

V7X_LANES = 128
V7X_SUBLANES = 8
V7X_BF16_ROWS = 16
V7X_VMEM_LIMIT = 58 * 1024 * 1024

MM_ROWS_CAP = 1056
MM1_COLS = 512
MM2_K = 512
LN_ROWS_CAP = 264
MOE_CHUNK = 2560
MOE_ROWBLK = 2560
MOE_SUB = 256
MOE_ALIGN = 128
MOE_COLS = 512
MOE_K = 512
ROW_SIZES_CHUNK = (1024, 1024, 512, 256, 128)
ROW_SIZES_ROWBLK = ROW_SIZES_CHUNK
CONV_ROWS = 256
CONV_HALO = 32
CONV_TAIL = V7X_SUBLANES
CONV_COLS = 128
CONV_ROW_CHUNK = 128
COMBINE_ROWS = 256
DMA_UNROLL = 8
NEG_INF = float("-inf")


def _pick_tile(n, cap, mult):
    best = None
    for d in range(mult, min(n, cap) + 1, mult):
        if n % d == 0:
            best = d
    if best is None:
        raise ValueError(f"no tile for {n} (cap {cap}, multiple of {mult})")
    return best


def _params(semantics):
    return pltpu.CompilerParams(dimension_semantics=semantics,
                                vmem_limit_bytes=V7X_VMEM_LIMIT)


def _silu(x):
    return x * jax.nn.sigmoid(x)


def _layer_norm_rows(z, g, b):
    mu = jnp.mean(z, axis=-1, keepdims=True)
    d = z - mu
    var = jnp.mean(d * d, axis=-1, keepdims=True)
    return d * lax.rsqrt(var + LN_EPS) * g + b


def _mm1_pair_kernel(*refs, mode, has_bias):
    if has_bias:
        x_ref, wa_ref, wb_ref, ba_ref, bb_ref, o_ref = refs
    else:
        x_ref, wa_ref, wb_ref, o_ref = refs
    x = x_ref[...]
    a = jnp.dot(x, wa_ref[...].astype(BF16), preferred_element_type=F32)
    b = jnp.dot(x, wb_ref[...].astype(BF16), preferred_element_type=F32)
    if has_bias:
        a = a + ba_ref[...]
        b = b + bb_ref[...]
    if mode == "glu":
        r = a * jax.nn.sigmoid(b)
    else:
        r = _silu(a) * b
    o_ref[...] = r.astype(o_ref.dtype)


def _row_vec(stack):
    return stack.reshape(stack.shape[0], 1, stack.shape[1])


def _mm1_pair(x, w, layer, bias, mode, out_dtype, name):
    m, k = x.shape
    h = w.shape[2] // 2
    tm = _pick_tile(m, MM_ROWS_CAP, V7X_BF16_ROWS)
    tn = _pick_tile(h, MM1_COLS, V7X_LANES)
    nj = h // tn
    in_specs = [
        pl.BlockSpec((tm, k), lambda i, j: (i, 0)),
        pl.BlockSpec((None, k, tn), lambda i, j: (layer, 0, j)),
        pl.BlockSpec((None, k, tn), lambda i, j: (layer, 0, j + nj)),
    ]
    args = [x, w, w]
    if bias is not None:
        b2 = _row_vec(bias)
        in_specs += [pl.BlockSpec((None, 1, tn), lambda i, j: (layer, 0, j)),
                     pl.BlockSpec((None, 1, tn), lambda i, j: (layer, 0, j + nj))]
        args += [b2, b2]
    return pl.pallas_call(
        functools.partial(_mm1_pair_kernel, mode=mode, has_bias=bias is not None),
        out_shape=jax.ShapeDtypeStruct((m, h), out_dtype),
        grid=(m // tm, nj),
        in_specs=in_specs,
        out_specs=pl.BlockSpec((tm, tn), lambda i, j: (i, j)),
        compiler_params=_params(("parallel", "arbitrary")),
        name=name,
    )(*args)


def _mm1_plain_kernel(x_ref, w_ref, o_ref):
    o_ref[...] = jnp.dot(x_ref[...], w_ref[...].astype(BF16),
                         preferred_element_type=F32).astype(o_ref.dtype)


def _mm1_plain(x, w, layer, out_dtype, name):
    m, k = x.shape
    n = w.shape[2]
    tm = _pick_tile(m, MM_ROWS_CAP, V7X_BF16_ROWS)
    tn = _pick_tile(n, MM1_COLS, V7X_LANES)
    return pl.pallas_call(
        _mm1_plain_kernel,
        out_shape=jax.ShapeDtypeStruct((m, n), out_dtype),
        grid=(m // tm, n // tn),
        in_specs=[pl.BlockSpec((tm, k), lambda i, j: (i, 0)),
                  pl.BlockSpec((None, k, tn), lambda i, j: (layer, 0, j))],
        out_specs=pl.BlockSpec((tm, tn), lambda i, j: (i, j)),
        compiler_params=_params(("parallel", "arbitrary")),
        name=name,
    )(x, w)


def _route_from_logits(logits):
    lane = lax.broadcasted_iota(jnp.int32, logits.shape, 1).astype(F32)
    no_lane = float(V7X_LANES)
    lg = jnp.where(lane < N_EXPERTS, logits, NEG_INF)
    m1 = jnp.max(lg, axis=-1, keepdims=True)
    i1 = jnp.min(jnp.where(lg == m1, lane, no_lane), axis=-1, keepdims=True)
    lg2 = jnp.where(lane == i1, NEG_INF, lg)
    m2 = jnp.max(lg2, axis=-1, keepdims=True)
    i2 = jnp.min(jnp.where(lg2 == m2, lane, no_lane), axis=-1, keepdims=True)
    e2 = jnp.exp(m2 - m1)
    den = 1.0 + e2
    w1 = 1.0 / den
    w2 = e2 / den
    out = jnp.where(lane == 0.0, i1,
                    jnp.where(lane == 1.0, i2,
                              jnp.where(lane == 2.0, w1,
                                        jnp.where(lane == 3.0, w2, 0.0))))
    return out


def _split_bf16(v):
    hi = v.astype(BF16)
    lo = (v - hi.astype(F32)).astype(BF16)
    return hi, lo


def _mm2_kernel(*refs, alpha, has_bias, has_router, emit_bf16, row_half, ln_rows):
    refs = list(refs)
    h_ref, w_ref, r_ref, g_ref, b_ref = refs[:5]
    pos = 5
    bias_ref = wr_ref = yb_ref = route_ref = None
    if has_bias:
        bias_ref = refs[pos]
        pos += 1
    if has_router:
        wr_ref = refs[pos]
        pos += 1
    y_ref = refs[pos]
    pos += 1
    if emit_bf16:
        yb_ref = refs[pos]
        pos += 1
    if has_router:
        route_ref = refs[pos]

    kk = pl.program_id(1)
    tm = y_ref.shape[0]

    @pl.when(kk == 0)
    def _():
        z0 = alpha * r_ref[...]
        if has_bias:
            z0 = z0 + bias_ref[...]
        y_ref[...] = z0

    wb = w_ref[...].astype(BF16)
    for s in range(tm // row_half):
        rows = pl.ds(s * row_half, row_half)
        y_ref[rows, :] += jnp.dot(h_ref[rows, :], wb, preferred_element_type=F32)

    @pl.when(kk == pl.num_programs(1) - 1)
    def _():
        g = g_ref[...]
        b = b_ref[...]
        if has_router:
            wr = wr_ref[...]
            wr_hi, wr_lo = _split_bf16(wr)

        def ln_chunk(c, carry):
            rows = pl.ds(pl.multiple_of(c * ln_rows, V7X_SUBLANES), ln_rows)
            y = _layer_norm_rows(y_ref[rows, :], g, b)
            y_ref[rows, :] = y
            if emit_bf16:
                yb_ref[rows, :] = y.astype(BF16)
            if has_router:
                y_hi, y_lo = _split_bf16(y)
                logits = (jnp.dot(y_hi, wr_hi, preferred_element_type=F32)
                          + jnp.dot(y_hi, wr_lo, preferred_element_type=F32)
                          + jnp.dot(y_lo, wr_hi, preferred_element_type=F32))
                route_ref[rows, :] = _route_from_logits(logits)
            return carry

        lax.fori_loop(0, tm // ln_rows, ln_chunk, 0)


def _mm2(h, w, w_layer, bias, resid, ln_g, ln_b, ln_layer, alpha, w_router, emit_bf16, name):
    m, k = h.shape
    d = w.shape[2]
    tm = _pick_tile(m, MM_ROWS_CAP, V7X_BF16_ROWS)
    tk = _pick_tile(k, MM2_K, V7X_LANES)
    row_half = tm // 2 if (tm // 2) % V7X_BF16_ROWS == 0 else tm
    ln_rows = _pick_tile(tm, LN_ROWS_CAP, V7X_SUBLANES)
    once = pl.Buffered(1)
    in_specs = [
        pl.BlockSpec((tm, tk), lambda i, kk: (i, kk)),
        pl.BlockSpec((None, tk, d), lambda i, kk: (w_layer, kk, 0)),
        pl.BlockSpec((tm, d), lambda i, kk: (i, 0), pipeline_mode=once),
        pl.BlockSpec((None, 1, d), lambda i, kk: (ln_layer, 0, 0)),
        pl.BlockSpec((None, 1, d), lambda i, kk: (ln_layer, 0, 0)),
    ]
    args = [h, w, resid, _row_vec(ln_g), _row_vec(ln_b)]
    if bias is not None:
        in_specs.append(pl.BlockSpec((None, 1, d), lambda i, kk: (w_layer, 0, 0)))
        args.append(_row_vec(bias))
    out_shape = [jax.ShapeDtypeStruct((m, d), F32)]
    out_specs = [pl.BlockSpec((tm, d), lambda i, kk: (i, 0))]
    if emit_bf16:
        out_shape.append(jax.ShapeDtypeStruct((m, d), BF16))
        out_specs.append(pl.BlockSpec((tm, d), lambda i, kk: (i, 0)))
    if w_router is not None:
        wr = jnp.pad(w_router, ((0, 0), (0, 0), (0, V7X_LANES - w_router.shape[2])))
        in_specs.append(pl.BlockSpec((None, d, V7X_LANES), lambda i, kk: (w_layer, 0, 0)))
        args.append(wr)
        out_shape.append(jax.ShapeDtypeStruct((m, V7X_LANES), F32))
        out_specs.append(pl.BlockSpec((tm, V7X_LANES), lambda i, kk: (i, 0)))
    return pl.pallas_call(
        functools.partial(_mm2_kernel, alpha=alpha, has_bias=bias is not None,
                          has_router=w_router is not None, emit_bf16=emit_bf16,
                          row_half=row_half, ln_rows=ln_rows),
        out_shape=out_shape,
        grid=(m // tm, k // tk),
        in_specs=in_specs,
        out_specs=out_specs,
        compiler_params=_params(("parallel", "arbitrary")),
        name=name,
    )(*args)


def _conv_ln_silu(ext_ref, pre_ref, w_ref, bdw_ref, g_ref, b_ref, o_ref, t_rows):
    c_total = pre_ref.shape[1]
    cols_blk = min(CONV_COLS, c_total)
    row_chunk = min(CONV_ROW_CHUNK, t_rows)
    lead = CONV_HALO - CONV_HIST
    n = row_chunk + V7X_SUBLANES

    def col_body(c, carry):
        cols = pl.ds(pl.multiple_of(c * cols_blk, V7X_LANES), cols_blk)
        bdw = bdw_ref[:, cols]
        for rc in range(t_rows // row_chunk):
            t0 = rc * row_chunk
            p = None
            for r in reversed(range(V7X_SUBLANES)):
                v = None
                for a in range((lead + CONV_WIDTH - 1 - r) // V7X_SUBLANES + 1):
                    k = V7X_SUBLANES * a + r - lead
                    if k < 0:
                        continue
                    term = w_ref[pl.ds(k, 1), cols] * ext_ref[
                        pl.ds(t0 + V7X_SUBLANES * a, n), cols]
                    v = term if v is None else v + term
                p = v if p is None else v + pltpu.roll(p, n - 1, axis=0)
            pre_ref[pl.ds(t0, row_chunk), cols] = p[:row_chunk] + bdw
        return carry

    lax.fori_loop(0, c_total // cols_blk, col_body, 0)
    y = _layer_norm_rows(pre_ref[...], g_ref[...], b_ref[...])
    o_ref[...] = _silu(y).astype(o_ref.dtype)


def _conv_prompt_kernel(prev_ref, cur_ref, w_ref, bdw_ref, g_ref, b_ref, o_ref,
                        ext_ref, pre_ref):
    i = pl.program_id(1)
    t_rows = cur_ref.shape[0]
    prev = prev_ref[...]
    ext_ref[pl.ds(0, CONV_HALO), :] = jnp.where(i == 0, jnp.zeros_like(prev), prev)
    ext_ref[pl.ds(CONV_HALO, t_rows), :] = cur_ref[...]
    ext_ref[pl.ds(CONV_HALO + t_rows, CONV_TAIL), :] = jnp.zeros(
        (CONV_TAIL, ext_ref.shape[1]), ext_ref.dtype)
    _conv_ln_silu(ext_ref, pre_ref, w_ref, bdw_ref, g_ref, b_ref, o_ref, t_rows)


def _conv_param_specs(c, layer):
    def fixed(*_):
        return (layer, 0, 0)
    return [pl.BlockSpec((None, CONV_WIDTH, c), fixed),
            pl.BlockSpec((None, 1, c), fixed),
            pl.BlockSpec((None, 1, c), fixed),
            pl.BlockSpec((None, 1, c), fixed)]


def _conv_prompt(u, batch, seq, layer, w_dw, b_dw, g, b):
    rows, c = u.shape
    t = _pick_tile(seq, CONV_ROWS, CONV_HALO)
    nt = seq // t
    halo_per_t = t // CONV_HALO

    def prev_map(bi, i):
        return (jnp.maximum(bi * (seq // CONV_HALO) + i * halo_per_t - 1, 0), 0)

    return pl.pallas_call(
        _conv_prompt_kernel,
        out_shape=jax.ShapeDtypeStruct((rows, c), BF16),
        grid=(batch, nt),
        in_specs=[
            pl.BlockSpec((CONV_HALO, c), prev_map),
            pl.BlockSpec((t, c), lambda bi, i: (bi * nt + i, 0)),
        ] + _conv_param_specs(c, layer),
        out_specs=pl.BlockSpec((t, c), lambda bi, i: (bi * nt + i, 0)),
        scratch_shapes=[pltpu.VMEM((t + CONV_HALO + CONV_TAIL, c), F32),
                        pltpu.VMEM((t, c), F32)],
        compiler_params=_params(("parallel", "arbitrary")),
        name="conv_prompt",
    )(u, u, w_dw, _row_vec(b_dw), _row_vec(g), _row_vec(b))


def _conv_sample_kernel(ext_ref, w_ref, bdw_ref, g_ref, b_ref, full_ref, o_ref, pre_ref):
    del full_ref
    _conv_ln_silu(ext_ref, pre_ref, w_ref, bdw_ref, g_ref, b_ref, o_ref, o_ref.shape[0])


def _conv_sample(ext, full, row0, layer, w_dw, b_dw, g, b):
    s, rows, c = ext.shape
    t = rows - CONV_HIST
    assert row0 % t == 0
    lead = CONV_HALO - CONV_HIST
    ext = jnp.pad(ext, ((0, 0), (lead, CONV_TAIL), (0, 0)))
    return pl.pallas_call(
        _conv_sample_kernel,
        out_shape=jax.ShapeDtypeStruct(full.shape, full.dtype),
        grid=(s,),
        in_specs=[pl.BlockSpec((None, rows + lead + CONV_TAIL, c), lambda i: (i, 0, 0))]
        + _conv_param_specs(c, layer) + [pl.BlockSpec(memory_space=pl.ANY)],
        out_specs=pl.BlockSpec((t, c), lambda i: (row0 // t + i, 0)),
        scratch_shapes=[pltpu.VMEM((t, c), F32)],
        input_output_aliases={5: 0},
        compiler_params=_params(("parallel",)),
        name="conv_sample",
    )(ext, w_dw, _row_vec(b_dw), _row_vec(g), _row_vec(b), full)


def _attend(q, k, v, mask, sink):
    s = lax.dot_general(q, k, (((1,), (1,)), ((), ())), preferred_element_type=F32)
    s = jnp.where(mask, s, NEG_INF)
    m = jnp.maximum(jnp.max(s, axis=-1, keepdims=True), sink)
    p = jnp.exp(s - m)
    den = jnp.sum(p, axis=-1, keepdims=True) + jnp.exp(sink - m)
    o = jnp.dot(p.astype(BF16), v, preferred_element_type=F32)
    return o / den


def _attn_prompt_kernel(sink_ref, q_ref, kp_ref, kc_ref, vp_ref, vc_ref, o_ref, *, hd):
    n = pl.program_id(1)
    blk = q_ref.shape[0]
    scale = hd ** -0.5
    rows = GROUP * blk
    band = (GROUP, blk, 2 * blk)
    qi = lax.broadcasted_iota(jnp.int32, band, 1).reshape(rows, 2 * blk)
    ci = lax.broadcasted_iota(jnp.int32, band, 2).reshape(rows, 2 * blk)
    gi = lax.broadcasted_iota(jnp.int32, (GROUP, blk, 1), 0).reshape(rows, 1)
    mask = (ci >= qi) & (ci <= qi + WINDOW) & ((ci >= blk) | (n > 0))
    kband = jnp.concatenate([kp_ref[...], kc_ref[...]], axis=0).astype(BF16)
    vband = jnp.concatenate([vp_ref[...], vc_ref[...]], axis=0).astype(BF16)
    for h in range(N_KV_HEADS):
        kh = kband[:, h * hd:(h + 1) * hd]
        vh = vband[:, h * hd:(h + 1) * hd]
        heads = [h * GROUP + g for g in range(GROUP)]
        q = jnp.concatenate([q_ref[:, hh * hd:(hh + 1) * hd] for hh in heads], axis=0)
        q = (q * scale).astype(BF16)
        sink = jnp.zeros((rows, 1), F32)
        for g, hh in enumerate(heads):
            sink = jnp.where(gi == g, sink_ref[hh], sink)
        o = _attend(q, kh, vh, mask, sink).astype(o_ref.dtype)
        for g, hh in enumerate(heads):
            o_ref[:, hh * hd:(hh + 1) * hd] = o[g * blk:(g + 1) * blk]


def _attn_prompt(qkv, sinks, batch, seq, d_model):
    hd = d_model // N_HEADS
    kvw = N_KV_HEADS * hd
    blk = WINDOW
    nb = seq // blk
    qcols = d_model // kvw

    def cur(off):
        return lambda bi, n, s: (bi * nb + n, off)

    def prev(off):
        return lambda bi, n, s: (bi * nb + jnp.maximum(n - 1, 0), off)

    grid_spec = pltpu.PrefetchScalarGridSpec(
        num_scalar_prefetch=1,
        grid=(batch, nb),
        in_specs=[
            pl.BlockSpec((blk, d_model), lambda bi, n, s: (bi * nb + n, 0)),
            pl.BlockSpec((blk, kvw), prev(qcols)),
            pl.BlockSpec((blk, kvw), cur(qcols)),
            pl.BlockSpec((blk, kvw), prev(qcols + 1)),
            pl.BlockSpec((blk, kvw), cur(qcols + 1)),
        ],
        out_specs=pl.BlockSpec((blk, d_model), lambda bi, n, s: (bi * nb + n, 0)),
    )
    return pl.pallas_call(
        functools.partial(_attn_prompt_kernel, hd=hd),
        out_shape=jax.ShapeDtypeStruct((qkv.shape[0], d_model), BF16),
        grid_spec=grid_spec,
        compiler_params=_params(("parallel", "arbitrary")),
        name="attn_prompt",
    )(sinks, qkv, qkv, qkv, qkv, qkv)


def _attn_sample_kernel(sink_ref, q_ref, kn_ref, vn_ref, kc_ref, vc_ref, full_ref, o_ref, *, hd):
    del full_ref
    scale = hd ** -0.5
    t_new = q_ref.shape[0]
    w_buf = kc_ref.shape[1]
    rows = GROUP * t_new
    s_len = w_buf + t_new
    grid3 = (GROUP, t_new, s_len)
    ti = lax.broadcasted_iota(jnp.int32, grid3, 1).reshape(rows, s_len)
    ci = lax.broadcasted_iota(jnp.int32, grid3, 2).reshape(rows, s_len)
    gi = lax.broadcasted_iota(jnp.int32, (GROUP, t_new, 1), 0).reshape(rows, 1)
    mask = (ci >= ti + (w_buf - WINDOW)) & (ci <= ti + w_buf)
    for h in range(N_KV_HEADS):
        cols = slice(h * hd, (h + 1) * hd)
        k = jnp.concatenate([kc_ref[h], kn_ref[:, cols]], axis=0).astype(BF16)
        v = jnp.concatenate([vc_ref[h], vn_ref[:, cols]], axis=0).astype(BF16)
        heads = [h * GROUP + g for g in range(GROUP)]
        q = jnp.concatenate([q_ref[:, hh * hd:(hh + 1) * hd] for hh in heads], axis=0)
        q = (q * scale).astype(BF16)
        sink = jnp.zeros((rows, 1), F32)
        for g, hh in enumerate(heads):
            sink = jnp.where(gi == g, sink_ref[hh], sink)
        o = _attend(q, k, v, mask, sink).astype(o_ref.dtype)
        for g, hh in enumerate(heads):
            o_ref[:, hh * hd:(hh + 1) * hd] = o[g * t_new:(g + 1) * t_new]


def _attn_sample(qkv, full, row0, n_seq, t_new, k_buf, v_buf, sinks, d_model):
    hd = d_model // N_HEADS
    kvw = N_KV_HEADS * hd
    w_buf = k_buf.shape[1]
    assert row0 % t_new == 0
    r0 = row0 // t_new
    qcols = d_model // kvw
    heads_first = lambda a: a.transpose(0, 2, 1, 3)
    cache = pl.BlockSpec((None, N_KV_HEADS, w_buf, hd), lambda i, sk: (i, 0, 0, 0))
    grid_spec = pltpu.PrefetchScalarGridSpec(
        num_scalar_prefetch=1,
        grid=(n_seq,),
        in_specs=[
            pl.BlockSpec((t_new, d_model), lambda i, sk: (r0 + i, 0)),
            pl.BlockSpec((t_new, kvw), lambda i, sk: (r0 + i, qcols)),
            pl.BlockSpec((t_new, kvw), lambda i, sk: (r0 + i, qcols + 1)),
            cache, cache,
            pl.BlockSpec(memory_space=pl.ANY),
        ],
        out_specs=pl.BlockSpec((t_new, d_model), lambda i, sk: (r0 + i, 0)),
    )
    return pl.pallas_call(
        functools.partial(_attn_sample_kernel, hd=hd),
        out_shape=jax.ShapeDtypeStruct(full.shape, full.dtype),
        grid_spec=grid_spec,
        input_output_aliases={6: 0},
        compiler_params=_params(("parallel",)),
        name="attn_sample",
    )(sinks, qkv, qkv, qkv, heads_first(k_buf), heads_first(v_buf), full)


def _compact(active, size):
    order = jnp.argsort(jnp.logical_not(active).astype(jnp.int32), stable=True)
    order = order[:size].astype(jnp.int32)
    return order, jnp.sum(active).astype(jnp.int32)


def _route_plan(i1, i2, n_tok):
    n_asg = 2 * n_tok
    halves = MOE_CHUNK // MOE_ROWBLK
    subs = MOE_CHUNK // MOE_SUB
    w_max = n_asg // MOE_CHUNK + N_EXPERTS
    r_max = n_asg // MOE_ROWBLK + N_EXPERTS
    s_max = n_asg // MOE_SUB + N_EXPERTS

    ei = jnp.stack([i1, i2], axis=1).reshape(-1)
    onehot = (ei[:, None] == jnp.arange(N_EXPERTS, dtype=jnp.int32)[None, :]).astype(jnp.int32)
    csum = jnp.cumsum(onehot, axis=0)
    counts = csum[-1]
    rank = jnp.take_along_axis(csum, ei[:, None], axis=1)[:, 0] - 1
    nch = (counts + MOE_CHUNK - 1) // MOE_CHUNK
    cend = jnp.cumsum(nch)
    cstart = cend - nch
    slot = (cstart[ei] * MOE_CHUNK + rank).astype(jnp.int32)
    n_chunks = cend[-1]

    wid = jnp.arange(w_max, dtype=jnp.int32)
    wcl = jnp.minimum(wid, n_chunks - 1)
    ce = jnp.sum((cend[None, :] <= wcl[:, None]).astype(jnp.int32), axis=1)
    ce = jnp.minimum(ce, N_EXPERTS - 1).astype(jnp.int32)
    crow = jnp.clip(counts[ce] - (wcl - cstart[ce]) * MOE_CHUNK, 0, MOE_CHUNK)
    crow = jnp.where(wid < n_chunks, crow, 0).astype(jnp.int32)

    def round_up(v, mlt):
        return ((v + mlt - 1) // mlt) * mlt

    rb_rows = jnp.clip(crow[:, None] - jnp.arange(halves, dtype=jnp.int32)[None, :] * MOE_ROWBLK,
                       0, MOE_ROWBLK).reshape(-1)
    rb_ids, n_rb = _compact(rb_rows > 0, r_max)
    rb_ids = rb_ids[jnp.minimum(jnp.arange(r_max), n_rb - 1)]
    rb_n = jnp.where(jnp.arange(r_max) < n_rb, round_up(rb_rows[rb_ids], MOE_ALIGN), 0)
    rb_e = ce[rb_ids // halves]

    sb_rows = jnp.clip(crow[:, None] - jnp.arange(subs, dtype=jnp.int32)[None, :] * MOE_SUB,
                       0, MOE_SUB).reshape(-1)
    sb_ids, n_sb = _compact(sb_rows > 0, s_max)
    sb_ids = sb_ids[jnp.minimum(jnp.arange(s_max), n_sb - 1)]
    src = jnp.zeros((w_max * MOE_CHUNK,), jnp.int32).at[slot].set(
        jnp.arange(n_asg, dtype=jnp.int32) // 2)
    src_sb = src.reshape(w_max * subs, MOE_SUB)[sb_ids]

    return dict(
        slot=slot, w_max=w_max,
        chunk_e=ce, chunk_blk=wcl.astype(jnp.int32),
        chunk_n=round_up(crow, MOE_ALIGN).astype(jnp.int32),
        rb_e=rb_e.astype(jnp.int32), rb_blk=rb_ids.astype(jnp.int32),
        rb_n=rb_n.astype(jnp.int32),
        sb_blk=sb_ids.astype(jnp.int32), n_sb=n_sb.reshape(1),
        src_sb=src_sb.reshape(s_max, 1, MOE_SUB).astype(jnp.int32),
    )


def _row_copy(src_hbm, row, dst_ref, dst_row, sem):
    return pltpu.make_async_copy(src_hbm.at[pl.ds(row, 1)], dst_ref.at[pl.ds(dst_row, 1)], sem)


def _start_rows(src_hbm, idx_ref, stride, offset, dst_ref, n_rows, sem):
    def body(c, carry):
        for u in range(DMA_UNROLL):
            r = c * DMA_UNROLL + u
            _row_copy(src_hbm, idx_ref[0, 0, stride * r + offset], dst_ref, r, sem).start()
        return carry

    lax.fori_loop(0, n_rows // DMA_UNROLL, body, 0)


def _wait_rows(src_hbm, dst_ref, n_rows, sem):
    def body(c, carry):
        for u in range(DMA_UNROLL):
            _row_copy(src_hbm, 0, dst_ref, c * DMA_UNROLL + u, sem).wait()
        return carry

    lax.fori_loop(0, n_rows // DMA_UNROLL, body, 0)


def _gather_kernel(sb_ref, nsb_ref, first_ref, next_ref, x_hbm, o_ref, buf_ref, sem):
    i = pl.program_id(0)
    n_sb = nsb_ref[0]
    n_rows = buf_ref.shape[1]
    slot = i % 2

    @pl.when((i == 0) & (n_sb > 0))
    def _():
        _start_rows(x_hbm, first_ref, 1, 0, buf_ref.at[0], n_rows, sem.at[0])

    @pl.when(i + 1 < n_sb)
    def _():
        _start_rows(x_hbm, next_ref, 1, 0, buf_ref.at[1 - slot], n_rows, sem.at[1 - slot])

    @pl.when(i < n_sb)
    def _():
        _wait_rows(x_hbm, buf_ref.at[slot], n_rows, sem.at[slot])
        o_ref[...] = buf_ref[slot].astype(o_ref.dtype)


def _moe_gather(x, plan):
    d = x.shape[1]
    s_max = plan["sb_blk"].shape[0]
    smem_blk = lambda index_map: pl.BlockSpec((1, 1, MOE_SUB), index_map,
                                              memory_space=pltpu.SMEM)
    grid_spec = pltpu.PrefetchScalarGridSpec(
        num_scalar_prefetch=2,
        grid=(s_max,),
        in_specs=[
            smem_blk(lambda i, sb, ns: (0, 0, 0)),
            smem_blk(lambda i, sb, ns: (jnp.minimum(i + 1, s_max - 1), 0, 0)),
            pl.BlockSpec(memory_space=pl.ANY),
        ],
        out_specs=pl.BlockSpec((MOE_SUB, d), lambda i, sb, ns: (sb[i], 0)),
        scratch_shapes=[pltpu.VMEM((2, MOE_SUB, d), F32), pltpu.SemaphoreType.DMA((2,))],
    )
    return pl.pallas_call(
        _gather_kernel,
        out_shape=jax.ShapeDtypeStruct((plan["w_max"] * MOE_CHUNK, d), BF16),
        grid_spec=grid_spec,
        compiler_params=_params(("arbitrary",)),
        name="moe_gather",
    )(plan["sb_blk"], plan["n_sb"], plan["src_sb"], plan["src_sb"], x)


def _for_each_row_block(n_rows, sizes, fn):
    off = jnp.int32(0)
    for size in sizes:
        take = (n_rows - off) >= size

        @pl.when(take)
        def _(off=off, size=size):
            fn(pl.multiple_of(off, MOE_ALIGN), size)

        off = off + jnp.where(take, size, 0)


def _gmm1_kernel(ce_ref, cb_ref, cn_ref, x_ref, wg_ref, wu_ref, o_ref, wgb_ref, wub_ref):
    n = cn_ref[pl.program_id(0)]

    @pl.when(n > 0)
    def _():
        wgb_ref[...] = wg_ref[...].astype(BF16)
        wub_ref[...] = wu_ref[...].astype(BF16)

        def blk(off, size):
            rows = pl.ds(off, size)
            x = x_ref[rows, :]
            g = jnp.dot(x, wgb_ref[...], preferred_element_type=F32)
            u = jnp.dot(x, wub_ref[...], preferred_element_type=F32)
            o_ref[rows, :] = (_silu(g) * u).astype(o_ref.dtype)

        _for_each_row_block(n, ROW_SIZES_CHUNK, blk)

        def zero(r, carry):
            rows = pl.ds(pl.multiple_of(r * MOE_ALIGN, MOE_ALIGN), MOE_ALIGN)
            o_ref[rows, :] = jnp.zeros((MOE_ALIGN, o_ref.shape[1]), o_ref.dtype)
            return carry

        lax.fori_loop(n // MOE_ALIGN, o_ref.shape[0] // MOE_ALIGN, zero, 0)


def _moe_gmm1(xs, w_gu, layer, plan):
    rows, d = xs.shape
    f = w_gu.shape[3] // 2
    tn = _pick_tile(f, MOE_COLS, V7X_LANES)
    nj = f // tn
    w_max = plan["w_max"]

    def col(w, j, cn):
        return jnp.where(cn[w] > 0, j, nj - 1)

    grid_spec = pltpu.PrefetchScalarGridSpec(
        num_scalar_prefetch=3,
        grid=(w_max, nj),
        in_specs=[
            pl.BlockSpec((MOE_CHUNK, d), lambda w, j, ce, cb, cn: (cb[w], 0)),
            pl.BlockSpec((None, None, d, tn),
                         lambda w, j, ce, cb, cn: (layer, ce[w], 0, col(w, j, cn))),
            pl.BlockSpec((None, None, d, tn),
                         lambda w, j, ce, cb, cn: (layer, ce[w], 0, col(w, j, cn) + nj)),
        ],
        out_specs=pl.BlockSpec((MOE_CHUNK, tn), lambda w, j, ce, cb, cn: (cb[w], col(w, j, cn))),
        scratch_shapes=[pltpu.VMEM((d, tn), BF16), pltpu.VMEM((d, tn), BF16)],
    )
    return pl.pallas_call(
        _gmm1_kernel,
        out_shape=jax.ShapeDtypeStruct((rows, f), BF16),
        grid_spec=grid_spec,
        compiler_params=_params(("arbitrary", "arbitrary")),
        name="moe_gmm1",
    )(plan["chunk_e"], plan["chunk_blk"], plan["chunk_n"], xs, w_gu, w_gu)


def _gmm2_kernel(re_ref, rb_ref, rn_ref, h_ref, w_ref, o_ref, wb_ref):
    n = rn_ref[pl.program_id(0)]
    kk = pl.program_id(1)

    @pl.when(n > 0)
    def _():
        @pl.when(kk == 0)
        def _():
            o_ref[...] = jnp.zeros_like(o_ref)

        wb_ref[...] = w_ref[...].astype(BF16)

        def blk(off, size):
            rows = pl.ds(off, size)
            o_ref[rows, :] += jnp.dot(h_ref[rows, :], wb_ref[...],
                                      preferred_element_type=F32)

        _for_each_row_block(n, ROW_SIZES_ROWBLK, blk)


def _moe_gmm2(hs, w_down, layer, plan):
    rows, f = hs.shape
    d = w_down.shape[3]
    tk = _pick_tile(f, MOE_K, V7X_LANES)
    nk = f // tk
    r_max = plan["rb_blk"].shape[0]

    def kblk(r, kk, rn):
        return jnp.where(rn[r] > 0, kk, nk - 1)

    grid_spec = pltpu.PrefetchScalarGridSpec(
        num_scalar_prefetch=3,
        grid=(r_max, nk),
        in_specs=[
            pl.BlockSpec((MOE_ROWBLK, tk), lambda r, kk, re, rb, rn: (rb[r], kblk(r, kk, rn))),
            pl.BlockSpec((None, None, tk, d),
                         lambda r, kk, re, rb, rn: (layer, re[r], kblk(r, kk, rn), 0)),
        ],
        out_specs=pl.BlockSpec((MOE_ROWBLK, d), lambda r, kk, re, rb, rn: (rb[r], 0),
                               pipeline_mode=pl.Buffered(1)),
        scratch_shapes=[pltpu.VMEM((tk, d), BF16)],
    )
    return pl.pallas_call(
        _gmm2_kernel,
        out_shape=jax.ShapeDtypeStruct((rows, d), F32),
        grid_spec=grid_spec,
        compiler_params=_params(("arbitrary", "arbitrary")),
        name="moe_gmm2",
    )(plan["rb_e"], plan["rb_blk"], plan["rb_n"], hs, w_down)


def _combine_kernel(*refs, alpha, emit_bf16):
    first_ref, next_ref, x_ref, route_ref, ys_hbm, g_ref, b_ref, y_ref = refs[:8]
    yb_ref = refs[8] if emit_bf16 else None
    buf_ref, sem = refs[-2:]
    i = pl.program_id(0)
    n_rows = x_ref.shape[0]
    slot = i % 2

    def start(idx_ref, s):
        for k in range(2):
            _start_rows(ys_hbm, idx_ref, 2, k, buf_ref.at[s, k], n_rows, sem.at[s])

    @pl.when(i == 0)
    def _():
        start(first_ref, 0)

    @pl.when(i + 1 < pl.num_programs(0))
    def _():
        start(next_ref, 1 - slot)

    for k in range(2):
        _wait_rows(ys_hbm, buf_ref.at[slot, k], n_rows, sem.at[slot])
    w1 = route_ref[:, 2:3]
    w2 = route_ref[:, 3:4]
    z = alpha * x_ref[...] + (w1 * buf_ref[slot, 0] + w2 * buf_ref[slot, 1])
    y = _layer_norm_rows(z, g_ref[...], b_ref[...])
    y_ref[...] = y
    if emit_bf16:
        yb_ref[...] = y.astype(BF16)


def _moe_combine(x, route, ys, plan, ln_g, ln_b, ln_layer, alpha, emit_bf16):
    n, d = x.shape
    tq = _pick_tile(n, COMBINE_ROWS, V7X_BF16_ROWS)
    nt = n // tq
    slots = plan["slot"].reshape(nt, 1, 2 * tq)
    smem_blk = lambda index_map: pl.BlockSpec((1, 1, 2 * tq), index_map,
                                              memory_space=pltpu.SMEM)
    out_shape = [jax.ShapeDtypeStruct((n, d), F32)]
    out_specs = [pl.BlockSpec((tq, d), lambda i: (i, 0))]
    if emit_bf16:
        out_shape.append(jax.ShapeDtypeStruct((n, d), BF16))
        out_specs.append(pl.BlockSpec((tq, d), lambda i: (i, 0)))
    return pl.pallas_call(
        functools.partial(_combine_kernel, alpha=alpha, emit_bf16=emit_bf16),
        out_shape=out_shape,
        grid=(nt,),
        in_specs=[
            smem_blk(lambda i: (0, 0, 0)),
            smem_blk(lambda i: (jnp.minimum(i + 1, nt - 1), 0, 0)),
            pl.BlockSpec((tq, d), lambda i: (i, 0)),
            pl.BlockSpec((tq, V7X_LANES), lambda i: (i, 0)),
            pl.BlockSpec(memory_space=pl.ANY),
            pl.BlockSpec((None, 1, d), lambda i: (ln_layer, 0, 0)),
            pl.BlockSpec((None, 1, d), lambda i: (ln_layer, 0, 0)),
        ],
        out_specs=out_specs,
        scratch_shapes=[pltpu.VMEM((2, 2, tq, d), F32), pltpu.SemaphoreType.DMA((2,))],
        compiler_params=_params(("arbitrary",)),
        name="moe_combine",
    )(slots, slots, x, route, ys, _row_vec(ln_g), _row_vec(ln_b))


def _moe_ffn(x, route, w_gu, w_down, layer, ln_g, ln_b, ln_layer, alpha, emit_bf16):
    n = x.shape[0]
    ids = route[:, :2].astype(jnp.int32)
    plan = _route_plan(ids[:, 0], ids[:, 1], n)
    xs = _moe_gather(x, plan)
    hs = _moe_gmm1(xs, w_gu, layer, plan)
    ys = _moe_gmm2(hs, w_down, layer, plan)
    return _moe_combine(x, route, ys, plan, ln_g, ln_b, ln_layer, alpha, emit_bf16)


def kernel(x_prompt, x_sample, state_conv, cache_k, cache_v, ln1_g, ln1_b, ln2_g, ln2_b,
           w_pw1, b_pw1, w_dw, b_dw, conv_norm_g, conv_norm_b, w_pw2, b_pw2,
           w_qkv, attn_sinks, w_o, w_ffn_gu, w_ffn_down, w_router, w_exp_gu, w_exp_down):
    batch, seq, d_model = x_prompt.shape
    dec_batch, dec_seq, _ = x_sample.shape
    depth = ln1_g.shape[0]
    alpha = float((2 * depth) ** 0.25)
    hd = d_model // N_HEADS
    kvw = N_KV_HEADS * hd
    n_p = batch * seq
    assert seq >= CONV_HIST and seq % WINDOW == 0

    x = jnp.concatenate([x_prompt.reshape(n_p, d_model),
                         x_sample.reshape(dec_batch * dec_seq, d_model)], axis=0)
    xb = x.astype(BF16)
    conv_p, conv_s, k_p, v_p, k_s, v_s = [], [], [], [], [], []

    def tail_rows(a, n_rows):
        return jnp.stack([a[(bi + 1) * seq - n_rows:(bi + 1) * seq] for bi in range(batch)])

    w_pw1, w_pw2, w_qkv, w_o = (w.astype(BF16) for w in (w_pw1, w_pw2, w_qkv, w_o))

    for i in range(depth):
        j = i // 2
        last = i == depth - 1
        if i % 2 == 0:
            u = _mm1_pair(xb, w_pw1, j, b_pw1, "glu", F32, f"pw1_glu_{i}")
            u_s = u[n_p:].reshape(dec_batch, dec_seq, u.shape[1])
            ext_s = jnp.concatenate([state_conv[j].astype(F32), u_s], axis=1)
            conv_p.append(tail_rows(u, CONV_HIST))
            conv_s.append(ext_s[:, -CONV_HIST:])
            c = _conv_prompt(u, batch, seq, j, w_dw, b_dw, conv_norm_g, conv_norm_b)
            c = _conv_sample(ext_s, c, n_p, j, w_dw, b_dw, conv_norm_g, conv_norm_b)
            x1, x1b = _mm2(c, w_pw2, j, b_pw2, x, ln1_g, ln1_b, i, alpha, None, True,
                           f"pw2_ln_{i}")
            hmid = _mm1_pair(x1b, w_ffn_gu, j, None, "swiglu", BF16, f"ffn_gu_{i}")
            x, *rest = _mm2(hmid, w_ffn_down, j, None, x1, ln2_g, ln2_b, i, alpha, None,
                            not last, f"ffn_down_ln_{i}")
            xb = rest[0] if rest else None
        else:
            qkv = _mm1_plain(xb, w_qkv, j, F32, f"qkv_{i}")
            kv_p = tail_rows(qkv[:, d_model:], WINDOW)
            kv_s = qkv[n_p:, d_model:].reshape(dec_batch, dec_seq, 2 * kvw)
            heads = lambda a: a.reshape(a.shape[0], a.shape[1], N_KV_HEADS, hd)
            w_buf = cache_k.shape[2]
            k_buf = cache_k[j].astype(F32)
            v_buf = cache_v[j].astype(F32)
            k_p.append(heads(kv_p[..., :kvw]))
            v_p.append(heads(kv_p[..., kvw:]))
            k_s.append(jnp.concatenate([k_buf, heads(kv_s[..., :kvw])], axis=1)[:, -w_buf:])
            v_s.append(jnp.concatenate([v_buf, heads(kv_s[..., kvw:])], axis=1)[:, -w_buf:])
            o = _attn_prompt(qkv, attn_sinks[j], batch, seq, d_model)
            o = _attn_sample(qkv, o, n_p, dec_batch, dec_seq, k_buf, v_buf, attn_sinks[j],
                             d_model)
            x1, route = _mm2(o, w_o, j, None, x, ln1_g, ln1_b, i, alpha, w_router, False,
                             f"wo_ln_router_{i}")
            x, *rest = _moe_ffn(x1, route, w_exp_gu, w_exp_down, j, ln2_g, ln2_b, i, alpha,
                                not last)
            xb = rest[0] if rest else None

    y_p = x[:n_p].reshape(batch, seq, d_model)
    y_s = x[n_p:].reshape(dec_batch, dec_seq, d_model)
    return (y_p, y_s, jnp.stack(conv_p), jnp.stack(conv_s),
            jnp.stack(k_p), jnp.stack(v_p), jnp.stack(k_s), jnp.stack(v_s))
```

```python
import functools

import jax
import jax.numpy as jnp
from jax import lax
from jax.experimental import pallas as pl
from jax.experimental.pallas import tpu as pltpu

F32 = jnp.float32
BF16 = jnp.bfloat16

N_HEADS = 32
N_KV_HEADS = 4
GROUP = N_HEADS // N_KV_HEADS
WINDOW = 128
CONV_WIDTH = 31
CONV_HIST = CONV_WIDTH - 1
N_EXPERTS = 8
LN_EPS = 1e-5

V7X_LANES = 128
V7X_SUBLANES = 8
V7X_BF16_ROWS = 16
V7X_VMEM_LIMIT = 58 * 1024 * 1024

MM_ROWS_CAP = 1056
MM1_COLS = 512
MM2_K = 512
MM2_RESIDENT_BYTES = 9 * 1024 * 1024
MM2_RESIDENT_ROWS_CAP = 528
LN_ROWS_CAP = 264
MOE_CHUNK = 2560
MOE_ROWBLK = 2560
MOE_SUB = 256
MOE_ALIGN = 128
MOE_COLS = 512
MOE_K = 512
ROW_SIZES_CHUNK = (1024, 1024, 512, 256, 128)
ROW_SIZES_ROWBLK = ROW_SIZES_CHUNK
CONV_ROWS = 256
CONV_HALO = 32
CONV_TAIL = V7X_SUBLANES
CONV_COLS = 128
CONV_ROW_CHUNK = 128
COMBINE_ROWS = 256
DMA_UNROLL = 8
NEG_INF = float("-inf")


def _pick_tile(n, cap, mult):
    best = None
    for d in range(mult, min(n, cap) + 1, mult):
        if n % d == 0:
            best = d
    if best is None:
        raise ValueError(f"no tile for {n} (cap {cap}, multiple of {mult})")
    return best


def _params(semantics):
    return pltpu.CompilerParams(dimension_semantics=semantics,
                                vmem_limit_bytes=V7X_VMEM_LIMIT)


def _silu(x):
    return x * jax.nn.sigmoid(x)


def _layer_norm_rows(z, g, b):
    mu = jnp.mean(z, axis=-1, keepdims=True)
    d = z - mu
    var = jnp.mean(d * d, axis=-1, keepdims=True)
    return d * lax.rsqrt(var + LN_EPS) * g + b


def _mm1_pair_kernel(*refs, mode, has_bias):
    if has_bias:
        x_ref, wa_ref, wb_ref, ba_ref, bb_ref, o_ref = refs
    else:
        x_ref, wa_ref, wb_ref, o_ref = refs
    x = x_ref[...]
    a = jnp.dot(x, wa_ref[...].astype(BF16), preferred_element_type=F32)
    b = jnp.dot(x, wb_ref[...].astype(BF16), preferred_element_type=F32)
    if has_bias:
        a = a + ba_ref[...]
        b = b + bb_ref[...]
    if mode == "glu":
        r = a * jax.nn.sigmoid(b)
    else:
        r = _silu(a) * b
    o_ref[...] = r.astype(o_ref.dtype)


def _row_vec(stack):
    return stack.reshape(stack.shape[0], 1, stack.shape[1])


def _mm1_pair(x, w, layer, bias, mode, out_dtype, name):
    m, k = x.shape
    h = w.shape[2] // 2
    tm = _pick_tile(m, MM_ROWS_CAP, V7X_BF16_ROWS)
    tn = _pick_tile(h, MM1_COLS, V7X_LANES)
    nj = h // tn
    in_specs = [
        pl.BlockSpec((tm, k), lambda i, j: (i, 0)),
        pl.BlockSpec((None, k, tn), lambda i, j: (layer, 0, j)),
        pl.BlockSpec((None, k, tn), lambda i, j: (layer, 0, j + nj)),
    ]
    args = [x, w, w]
    if bias is not None:
        b2 = _row_vec(bias)
        in_specs += [pl.BlockSpec((None, 1, tn), lambda i, j: (layer, 0, j)),
                     pl.BlockSpec((None, 1, tn), lambda i, j: (layer, 0, j + nj))]
        args += [b2, b2]
    return pl.pallas_call(
        functools.partial(_mm1_pair_kernel, mode=mode, has_bias=bias is not None),
        out_shape=jax.ShapeDtypeStruct((m, h), out_dtype),
        grid=(m // tm, nj),
        in_specs=in_specs,
        out_specs=pl.BlockSpec((tm, tn), lambda i, j: (i, j)),
        compiler_params=_params(("parallel", "arbitrary")),
        name=name,
    )(*args)


def _mm1_plain_kernel(x_ref, w_ref, o_ref):
    o_ref[...] = jnp.dot(x_ref[...], w_ref[...].astype(BF16),
                         preferred_element_type=F32).astype(o_ref.dtype)


def _mm1_plain(x, w, layer, out_dtype, name):
    m, k = x.shape
    n = w.shape[2]
    tm = _pick_tile(m, MM_ROWS_CAP, V7X_BF16_ROWS)
    tn = _pick_tile(n, MM1_COLS, V7X_LANES)
    return pl.pallas_call(
        _mm1_plain_kernel,
        out_shape=jax.ShapeDtypeStruct((m, n), out_dtype),
        grid=(m // tm, n // tn),
        in_specs=[pl.BlockSpec((tm, k), lambda i, j: (i, 0)),
                  pl.BlockSpec((None, k, tn), lambda i, j: (layer, 0, j))],
        out_specs=pl.BlockSpec((tm, tn), lambda i, j: (i, j)),
        compiler_params=_params(("parallel", "arbitrary")),
        name=name,
    )(x, w)


def _route_from_logits(logits):
    lane = lax.broadcasted_iota(jnp.int32, logits.shape, 1).astype(F32)
    no_lane = float(V7X_LANES)
    lg = jnp.where(lane < N_EXPERTS, logits, NEG_INF)
    m1 = jnp.max(lg, axis=-1, keepdims=True)
    i1 = jnp.min(jnp.where(lg == m1, lane, no_lane), axis=-1, keepdims=True)
    lg2 = jnp.where(lane == i1, NEG_INF, lg)
    m2 = jnp.max(lg2, axis=-1, keepdims=True)
    i2 = jnp.min(jnp.where(lg2 == m2, lane, no_lane), axis=-1, keepdims=True)
    e2 = jnp.exp(m2 - m1)
    den = 1.0 + e2
    w1 = 1.0 / den
    w2 = e2 / den
    out = jnp.where(lane == 0.0, i1,
                    jnp.where(lane == 1.0, i2,
                              jnp.where(lane == 2.0, w1,
                                        jnp.where(lane == 3.0, w2, 0.0))))
    return out


def _split_bf16(v):
    hi = v.astype(BF16)
    lo = (v - hi.astype(F32)).astype(BF16)
    return hi, lo


def _mm2_kernel(*refs, alpha, has_bias, has_router, emit_bf16, row_half, ln_rows):
    refs = list(refs)
    h_ref, w_ref, r_ref, g_ref, b_ref = refs[:5]
    pos = 5
    bias_ref = wr_ref = yb_ref = route_ref = None
    if has_bias:
        bias_ref = refs[pos]
        pos += 1
    if has_router:
        wr_ref = refs[pos]
        pos += 1
    y_ref = refs[pos]
    pos += 1
    if emit_bf16:
        yb_ref = refs[pos]
        pos += 1
    if has_router:
        route_ref = refs[pos]

    kk = pl.program_id(1)
    tm = y_ref.shape[0]

    @pl.when(kk == 0)
    def _():
        z0 = alpha * r_ref[...]
        if has_bias:
            z0 = z0 + bias_ref[...]
        y_ref[...] = z0

    wb = w_ref[...].astype(BF16)
    for s in range(tm // row_half):
        rows = pl.ds(s * row_half, row_half)
        y_ref[rows, :] += jnp.dot(h_ref[rows, :], wb, preferred_element_type=F32)

    @pl.when(kk == pl.num_programs(1) - 1)
    def _():
        g = g_ref[...]
        b = b_ref[...]
        if has_router:
            wr = wr_ref[...]
            wr_hi, wr_lo = _split_bf16(wr)

        def ln_chunk(c, carry):
            rows = pl.ds(pl.multiple_of(c * ln_rows, V7X_SUBLANES), ln_rows)
            y = _layer_norm_rows(y_ref[rows, :], g, b)
            y_ref[rows, :] = y
            if emit_bf16:
                yb_ref[rows, :] = y.astype(BF16)
            if has_router:
                y_hi, y_lo = _split_bf16(y)
                logits = (jnp.dot(y_hi, wr_hi, preferred_element_type=F32)
                          + jnp.dot(y_hi, wr_lo, preferred_element_type=F32)
                          + jnp.dot(y_lo, wr_hi, preferred_element_type=F32))
                route_ref[rows, :] = _route_from_logits(logits)
            return carry

        lax.fori_loop(0, tm // ln_rows, ln_chunk, 0)


def _mm2(h, w, w_layer, bias, resid, ln_g, ln_b, ln_layer, alpha, w_router, emit_bf16, name):
    m, k = h.shape
    d = w.shape[2]
    once = pl.Buffered(1)
    if k * d * w.dtype.itemsize <= MM2_RESIDENT_BYTES:
        tm = _pick_tile(m, MM2_RESIDENT_ROWS_CAP, V7X_BF16_ROWS)
        tk, row_half = k, tm
        w_mode, resid_mode = once, None
    else:
        tm = _pick_tile(m, MM_ROWS_CAP, V7X_BF16_ROWS)
        tk = _pick_tile(k, MM2_K, V7X_LANES)
        row_half = tm // 2 if (tm // 2) % V7X_BF16_ROWS == 0 else tm
        w_mode, resid_mode = None, once
    ln_rows = _pick_tile(tm, LN_ROWS_CAP, V7X_SUBLANES)
    in_specs = [
        pl.BlockSpec((tm, tk), lambda i, kk: (i, kk)),
        pl.BlockSpec((None, tk, d), lambda i, kk: (w_layer, kk, 0), pipeline_mode=w_mode),
        pl.BlockSpec((tm, d), lambda i, kk: (i, 0), pipeline_mode=resid_mode),
        pl.BlockSpec((None, 1, d), lambda i, kk: (ln_layer, 0, 0)),
        pl.BlockSpec((None, 1, d), lambda i, kk: (ln_layer, 0, 0)),
    ]
    args = [h, w, resid, _row_vec(ln_g), _row_vec(ln_b)]
    if bias is not None:
        in_specs.append(pl.BlockSpec((None, 1, d), lambda i, kk: (w_layer, 0, 0)))
        args.append(_row_vec(bias))
    out_shape = [jax.ShapeDtypeStruct((m, d), F32)]
    out_specs = [pl.BlockSpec((tm, d), lambda i, kk: (i, 0))]
    if emit_bf16:
        out_shape.append(jax.ShapeDtypeStruct((m, d), BF16))
        out_specs.append(pl.BlockSpec((tm, d), lambda i, kk: (i, 0)))
    if w_router is not None:
        wr = jnp.pad(w_router, ((0, 0), (0, 0), (0, V7X_LANES - w_router.shape[2])))
        in_specs.append(pl.BlockSpec((None, d, V7X_LANES), lambda i, kk: (w_layer, 0, 0)))
        args.append(wr)
        out_shape.append(jax.ShapeDtypeStruct((m, V7X_LANES), F32))
        out_specs.append(pl.BlockSpec((tm, V7X_LANES), lambda i, kk: (i, 0)))
    return pl.pallas_call(
        functools.partial(_mm2_kernel, alpha=alpha, has_bias=bias is not None,
                          has_router=w_router is not None, emit_bf16=emit_bf16,
                          row_half=row_half, ln_rows=ln_rows),
        out_shape=out_shape,
        grid=(m // tm, k // tk),
        in_specs=in_specs,
        out_specs=out_specs,
        compiler_params=_params(("parallel", "arbitrary")),
        name=name,
    )(*args)


def _conv_ln_silu(ext_ref, pre_ref, w_ref, bdw_ref, g_ref, b_ref, o_ref, t_rows):
    c_total = pre_ref.shape[1]
    cols_blk = min(CONV_COLS, c_total)
    row_chunk = min(CONV_ROW_CHUNK, t_rows)
    lead = CONV_HALO - CONV_HIST
    n = row_chunk + V7X_SUBLANES

    def col_body(c, carry):
        cols = pl.ds(pl.multiple_of(c * cols_blk, V7X_LANES), cols_blk)
        bdw = bdw_ref[:, cols]
        for rc in range(t_rows // row_chunk):
            t0 = rc * row_chunk
            p = None
            for r in reversed(range(V7X_SUBLANES)):
                v = None
                for a in range((lead + CONV_WIDTH - 1 - r) // V7X_SUBLANES + 1):
                    k = V7X_SUBLANES * a + r - lead
                    if k < 0:
                        continue
                    term = w_ref[pl.ds(k, 1), cols] * ext_ref[
                        pl.ds(t0 + V7X_SUBLANES * a, n), cols]
                    v = term if v is None else v + term
                p = v if p is None else v + pltpu.roll(p, n - 1, axis=0)
            pre_ref[pl.ds(t0, row_chunk), cols] = p[:row_chunk] + bdw
        return carry

    lax.fori_loop(0, c_total // cols_blk, col_body, 0)
    y = _layer_norm_rows(pre_ref[...], g_ref[...], b_ref[...])
    o_ref[...] = _silu(y).astype(o_ref.dtype)


def _conv_prompt_kernel(prev_ref, cur_ref, w_ref, bdw_ref, g_ref, b_ref, o_ref,
                        ext_ref, pre_ref):
    i = pl.program_id(1)
    t_rows = cur_ref.shape[0]
    prev = prev_ref[...]
    ext_ref[pl.ds(0, CONV_HALO), :] = jnp.where(i == 0, jnp.zeros_like(prev), prev)
    ext_ref[pl.ds(CONV_HALO, t_rows), :] = cur_ref[...]
    ext_ref[pl.ds(CONV_HALO + t_rows, CONV_TAIL), :] = jnp.zeros(
        (CONV_TAIL, ext_ref.shape[1]), ext_ref.dtype)
    _conv_ln_silu(ext_ref, pre_ref, w_ref, bdw_ref, g_ref, b_ref, o_ref, t_rows)


def _conv_param_specs(c, layer):
    def fixed(*_):
        return (layer, 0, 0)
    return [pl.BlockSpec((None, CONV_WIDTH, c), fixed),
            pl.BlockSpec((None, 1, c), fixed),
            pl.BlockSpec((None, 1, c), fixed),
            pl.BlockSpec((None, 1, c), fixed)]


def _conv_prompt(u, batch, seq, layer, w_dw, b_dw, g, b):
    rows, c = u.shape
    t = _pick_tile(seq, CONV_ROWS, CONV_HALO)
    nt = seq // t
    halo_per_t = t // CONV_HALO

    def prev_map(bi, i):
        return (jnp.maximum(bi * (seq // CONV_HALO) + i * halo_per_t - 1, 0), 0)

    return pl.pallas_call(
        _conv_prompt_kernel,
        out_shape=jax.ShapeDtypeStruct((rows, c), BF16),
        grid=(batch, nt),
        in_specs=[
            pl.BlockSpec((CONV_HALO, c), prev_map),
            pl.BlockSpec((t, c), lambda bi, i: (bi * nt + i, 0)),
        ] + _conv_param_specs(c, layer),
        out_specs=pl.BlockSpec((t, c), lambda bi, i: (bi * nt + i, 0)),
        scratch_shapes=[pltpu.VMEM((t + CONV_HALO + CONV_TAIL, c), F32),
                        pltpu.VMEM((t, c), F32)],
        compiler_params=_params(("parallel", "arbitrary")),
        name="conv_prompt",
    )(u, u, w_dw, _row_vec(b_dw), _row_vec(g), _row_vec(b))


def _conv_sample_kernel(ext_ref, w_ref, bdw_ref, g_ref, b_ref, full_ref, o_ref, pre_ref):
    del full_ref
    _conv_ln_silu(ext_ref, pre_ref, w_ref, bdw_ref, g_ref, b_ref, o_ref, o_ref.shape[0])


def _conv_sample(ext, full, row0, layer, w_dw, b_dw, g, b):
    s, rows, c = ext.shape
    t = rows - CONV_HIST
    assert row0 % t == 0
    lead = CONV_HALO - CONV_HIST
    ext = jnp.pad(ext, ((0, 0), (lead, CONV_TAIL), (0, 0)))
    return pl.pallas_call(
        _conv_sample_kernel,
        out_shape=jax.ShapeDtypeStruct(full.shape, full.dtype),
        grid=(s,),
        in_specs=[pl.BlockSpec((None, rows + lead + CONV_TAIL, c), lambda i: (i, 0, 0))]
        + _conv_param_specs(c, layer) + [pl.BlockSpec(memory_space=pl.ANY)],
        out_specs=pl.BlockSpec((t, c), lambda i: (row0 // t + i, 0)),
        scratch_shapes=[pltpu.VMEM((t, c), F32)],
        input_output_aliases={5: 0},
        compiler_params=_params(("parallel",)),
        name="conv_sample",
    )(ext, w_dw, _row_vec(b_dw), _row_vec(g), _row_vec(b), full)


def _attend(q, k, v, mask, sink):
    s = lax.dot_general(q, k, (((1,), (1,)), ((), ())), preferred_element_type=F32)
    s = jnp.where(mask, s, NEG_INF)
    m = jnp.maximum(jnp.max(s, axis=-1, keepdims=True), sink)
    p = jnp.exp(s - m)
    den = jnp.sum(p, axis=-1, keepdims=True) + jnp.exp(sink - m)
    o = jnp.dot(p.astype(BF16), v, preferred_element_type=F32)
    return o / den


def _attn_prompt_kernel(sink_ref, q_ref, kp_ref, kc_ref, vp_ref, vc_ref, o_ref, *, hd):
    n = pl.program_id(1)
    blk = q_ref.shape[0]
    scale = hd ** -0.5
    rows = GROUP * blk
    band = (GROUP, blk, 2 * blk)
    qi = lax.broadcasted_iota(jnp.int32, band, 1).reshape(rows, 2 * blk)
    ci = lax.broadcasted_iota(jnp.int32, band, 2).reshape(rows, 2 * blk)
    gi = lax.broadcasted_iota(jnp.int32, (GROUP, blk, 1), 0).reshape(rows, 1)
    mask = (ci >= qi) & (ci <= qi + WINDOW) & ((ci >= blk) | (n > 0))
    kband = jnp.concatenate([kp_ref[...], kc_ref[...]], axis=0).astype(BF16)
    vband = jnp.concatenate([vp_ref[...], vc_ref[...]], axis=0).astype(BF16)
    for h in range(N_KV_HEADS):
        kh = kband[:, h * hd:(h + 1) * hd]
        vh = vband[:, h * hd:(h + 1) * hd]
        heads = [h * GROUP + g for g in range(GROUP)]
        q = jnp.concatenate([q_ref[:, hh * hd:(hh + 1) * hd] for hh in heads], axis=0)
        q = (q * scale).astype(BF16)
        sink = jnp.zeros((rows, 1), F32)
        for g, hh in enumerate(heads):
            sink = jnp.where(gi == g, sink_ref[hh], sink)
        o = _attend(q, kh, vh, mask, sink).astype(o_ref.dtype)
        for g, hh in enumerate(heads):
            o_ref[:, hh * hd:(hh + 1) * hd] = o[g * blk:(g + 1) * blk]


def _attn_prompt(qkv, sinks, batch, seq, d_model):
    hd = d_model // N_HEADS
    kvw = N_KV_HEADS * hd
    blk = WINDOW
    nb = seq // blk
    qcols = d_model // kvw

    def cur(off):
        return lambda bi, n, s: (bi * nb + n, off)

    def prev(off):
        return lambda bi, n, s: (bi * nb + jnp.maximum(n - 1, 0), off)

    grid_spec = pltpu.PrefetchScalarGridSpec(
        num_scalar_prefetch=1,
        grid=(batch, nb),
        in_specs=[
            pl.BlockSpec((blk, d_model), lambda bi, n, s: (bi * nb + n, 0)),
            pl.BlockSpec((blk, kvw), prev(qcols)),
            pl.BlockSpec((blk, kvw), cur(qcols)),
            pl.BlockSpec((blk, kvw), prev(qcols + 1)),
            pl.BlockSpec((blk, kvw), cur(qcols + 1)),
        ],
        out_specs=pl.BlockSpec((blk, d_model), lambda bi, n, s: (bi * nb + n, 0)),
    )
    return pl.pallas_call(
        functools.partial(_attn_prompt_kernel, hd=hd),
        out_shape=jax.ShapeDtypeStruct((qkv.shape[0], d_model), BF16),
        grid_spec=grid_spec,
        compiler_params=_params(("parallel", "arbitrary")),
        name="attn_prompt",
    )(sinks, qkv, qkv, qkv, qkv, qkv)


def _attn_sample_kernel(sink_ref, q_ref, kn_ref, vn_ref, kc_ref, vc_ref, full_ref, o_ref, *, hd):
    del full_ref
    scale = hd ** -0.5
    t_new = q_ref.shape[0]
    w_buf = kc_ref.shape[1]
    rows = GROUP * t_new
    s_len = w_buf + t_new
    grid3 = (GROUP, t_new, s_len)
    ti = lax.broadcasted_iota(jnp.int32, grid3, 1).reshape(rows, s_len)
    ci = lax.broadcasted_iota(jnp.int32, grid3, 2).reshape(rows, s_len)
    gi = lax.broadcasted_iota(jnp.int32, (GROUP, t_new, 1), 0).reshape(rows, 1)
    mask = (ci >= ti + (w_buf - WINDOW)) & (ci <= ti + w_buf)
    for h in range(N_KV_HEADS):
        cols = slice(h * hd, (h + 1) * hd)
        k = jnp.concatenate([kc_ref[h], kn_ref[:, cols]], axis=0).astype(BF16)
        v = jnp.concatenate([vc_ref[h], vn_ref[:, cols]], axis=0).astype(BF16)
        heads = [h * GROUP + g for g in range(GROUP)]
        q = jnp.concatenate([q_ref[:, hh * hd:(hh + 1) * hd] for hh in heads], axis=0)
        q = (q * scale).astype(BF16)
        sink = jnp.zeros((rows, 1), F32)
        for g, hh in enumerate(heads):
            sink = jnp.where(gi == g, sink_ref[hh], sink)
        o = _attend(q, k, v, mask, sink).astype(o_ref.dtype)
        for g, hh in enumerate(heads):
            o_ref[:, hh * hd:(hh + 1) * hd] = o[g * t_new:(g + 1) * t_new]


def _attn_sample(qkv, full, row0, n_seq, t_new, k_buf, v_buf, sinks, d_model):
    hd = d_model // N_HEADS
    kvw = N_KV_HEADS * hd
    w_buf = k_buf.shape[1]
    assert row0 % t_new == 0
    r0 = row0 // t_new
    qcols = d_model // kvw
    heads_first = lambda a: a.transpose(0, 2, 1, 3)
    cache = pl.BlockSpec((None, N_KV_HEADS, w_buf, hd), lambda i, sk: (i, 0, 0, 0))
    grid_spec = pltpu.PrefetchScalarGridSpec(
        num_scalar_prefetch=1,
        grid=(n_seq,),
        in_specs=[
            pl.BlockSpec((t_new, d_model), lambda i, sk: (r0 + i, 0)),
            pl.BlockSpec((t_new, kvw), lambda i, sk: (r0 + i, qcols)),
            pl.BlockSpec((t_new, kvw), lambda i, sk: (r0 + i, qcols + 1)),
            cache, cache,
            pl.BlockSpec(memory_space=pl.ANY),
        ],
        out_specs=pl.BlockSpec((t_new, d_model), lambda i, sk: (r0 + i, 0)),
    )
    return pl.pallas_call(
        functools.partial(_attn_sample_kernel, hd=hd),
        out_shape=jax.ShapeDtypeStruct(full.shape, full.dtype),
        grid_spec=grid_spec,
        input_output_aliases={6: 0},
        compiler_params=_params(("parallel",)),
        name="attn_sample",
    )(sinks, qkv, qkv, qkv, heads_first(k_buf), heads_first(v_buf), full)


def _compact(active, size):
    order = jnp.argsort(jnp.logical_not(active).astype(jnp.int32), stable=True)
    order = order[:size].astype(jnp.int32)
    return order, jnp.sum(active).astype(jnp.int32)


def _route_plan(i1, i2, n_tok):
    n_asg = 2 * n_tok
    halves = MOE_CHUNK // MOE_ROWBLK
    subs = MOE_CHUNK // MOE_SUB
    w_max = n_asg // MOE_CHUNK + N_EXPERTS
    r_max = n_asg // MOE_ROWBLK + N_EXPERTS
    s_max = n_asg // MOE_SUB + N_EXPERTS

    ei = jnp.stack([i1, i2], axis=1).reshape(-1)
    onehot = (ei[:, None] == jnp.arange(N_EXPERTS, dtype=jnp.int32)[None, :]).astype(jnp.int32)
    csum = jnp.cumsum(onehot, axis=0)
    counts = csum[-1]
    rank = jnp.take_along_axis(csum, ei[:, None], axis=1)[:, 0] - 1
    nch = (counts + MOE_CHUNK - 1) // MOE_CHUNK
    cend = jnp.cumsum(nch)
    cstart = cend - nch
    slot = (cstart[ei] * MOE_CHUNK + rank).astype(jnp.int32)
    n_chunks = cend[-1]

    wid = jnp.arange(w_max, dtype=jnp.int32)
    wcl = jnp.minimum(wid, n_chunks - 1)
    ce = jnp.sum((cend[None, :] <= wcl[:, None]).astype(jnp.int32), axis=1)
    ce = jnp.minimum(ce, N_EXPERTS - 1).astype(jnp.int32)
    crow = jnp.clip(counts[ce] - (wcl - cstart[ce]) * MOE_CHUNK, 0, MOE_CHUNK)
    crow = jnp.where(wid < n_chunks, crow, 0).astype(jnp.int32)

    def round_up(v, mlt):
        return ((v + mlt - 1) // mlt) * mlt

    rb_rows = jnp.clip(crow[:, None] - jnp.arange(halves, dtype=jnp.int32)[None, :] * MOE_ROWBLK,
                       0, MOE_ROWBLK).reshape(-1)
    rb_ids, n_rb = _compact(rb_rows > 0, r_max)
    rb_ids = rb_ids[jnp.minimum(jnp.arange(r_max), n_rb - 1)]
    rb_n = jnp.where(jnp.arange(r_max) < n_rb, round_up(rb_rows[rb_ids], MOE_ALIGN), 0)
    rb_e = ce[rb_ids // halves]

    sb_rows = jnp.clip(crow[:, None] - jnp.arange(subs, dtype=jnp.int32)[None, :] * MOE_SUB,
                       0, MOE_SUB).reshape(-1)
    sb_ids, n_sb = _compact(sb_rows > 0, s_max)
    sb_ids = sb_ids[jnp.minimum(jnp.arange(s_max), n_sb - 1)]
    src = jnp.zeros((w_max * MOE_CHUNK,), jnp.int32).at[slot].set(
        jnp.arange(n_asg, dtype=jnp.int32) // 2)
    src_sb = src.reshape(w_max * subs, MOE_SUB)[sb_ids]

    return dict(
        slot=slot, w_max=w_max,
        chunk_e=ce, chunk_blk=wcl.astype(jnp.int32),
        chunk_n=round_up(crow, MOE_ALIGN).astype(jnp.int32),
        rb_e=rb_e.astype(jnp.int32), rb_blk=rb_ids.astype(jnp.int32),
        rb_n=rb_n.astype(jnp.int32),
        sb_blk=sb_ids.astype(jnp.int32), n_sb=n_sb.reshape(1),
        src_sb=src_sb.reshape(s_max, 1, MOE_SUB).astype(jnp.int32),
    )


def _row_copy(src_hbm, row, dst_ref, dst_row, sem):
    return pltpu.make_async_copy(src_hbm.at[pl.ds(row, 1)], dst_ref.at[pl.ds(dst_row, 1)], sem)


def _start_rows(src_hbm, idx_ref, stride, offset, dst_ref, n_rows, sem):
    def body(c, carry):
        for u in range(DMA_UNROLL):
            r = c * DMA_UNROLL + u
            _row_copy(src_hbm, idx_ref[0, 0, stride * r + offset], dst_ref, r, sem).start()
        return carry

    lax.fori_loop(0, n_rows // DMA_UNROLL, body, 0)


def _wait_rows(src_hbm, dst_ref, n_rows, sem):
    def body(c, carry):
        for u in range(DMA_UNROLL):
            _row_copy(src_hbm, 0, dst_ref, c * DMA_UNROLL + u, sem).wait()
        return carry

    lax.fori_loop(0, n_rows // DMA_UNROLL, body, 0)


def _gather_kernel(sb_ref, nsb_ref, first_ref, next_ref, x_hbm, o_ref, buf_ref, sem):
    i = pl.program_id(0)
    n_sb = nsb_ref[0]
    n_rows = buf_ref.shape[1]
    slot = i % 2

    @pl.when((i == 0) & (n_sb > 0))
    def _():
        _start_rows(x_hbm, first_ref, 1, 0, buf_ref.at[0], n_rows, sem.at[0])

    @pl.when(i + 1 < n_sb)
    def _():
        _start_rows(x_hbm, next_ref, 1, 0, buf_ref.at[1 - slot], n_rows, sem.at[1 - slot])

    @pl.when(i < n_sb)
    def _():
        _wait_rows(x_hbm, buf_ref.at[slot], n_rows, sem.at[slot])
        o_ref[...] = buf_ref[slot].astype(o_ref.dtype)


def _moe_gather(x, plan):
    d = x.shape[1]
    s_max = plan["sb_blk"].shape[0]
    smem_blk = lambda index_map: pl.BlockSpec((1, 1, MOE_SUB), index_map,
                                              memory_space=pltpu.SMEM)
    grid_spec = pltpu.PrefetchScalarGridSpec(
        num_scalar_prefetch=2,
        grid=(s_max,),
        in_specs=[
            smem_blk(lambda i, sb, ns: (0, 0, 0)),
            smem_blk(lambda i, sb, ns: (jnp.minimum(i + 1, s_max - 1), 0, 0)),
            pl.BlockSpec(memory_space=pl.ANY),
        ],
        out_specs=pl.BlockSpec((MOE_SUB, d), lambda i, sb, ns: (sb[i], 0)),
        scratch_shapes=[pltpu.VMEM((2, MOE_SUB, d), F32), pltpu.SemaphoreType.DMA((2,))],
    )
    return pl.pallas_call(
        _gather_kernel,
        out_shape=jax.ShapeDtypeStruct((plan["w_max"] * MOE_CHUNK, d), BF16),
        grid_spec=grid_spec,
        compiler_params=_params(("arbitrary",)),
        name="moe_gather",
    )(plan["sb_blk"], plan["n_sb"], plan["src_sb"], plan["src_sb"], x)


def _for_each_row_block(n_rows, sizes, fn):
    off = jnp.int32(0)
    for size in sizes:
        take = (n_rows - off) >= size

        @pl.when(take)
        def _(off=off, size=size):
            fn(pl.multiple_of(off, MOE_ALIGN), size)

        off = off + jnp.where(take, size, 0)


def _gmm1_kernel(ce_ref, cb_ref, cn_ref, x_ref, wg_ref, wu_ref, o_ref, wgb_ref, wub_ref):
    n = cn_ref[pl.program_id(0)]

    @pl.when(n > 0)
    def _():
        wgb_ref[...] = wg_ref[...].astype(BF16)
        wub_ref[...] = wu_ref[...].astype(BF16)

        def blk(off, size):
            rows = pl.ds(off, size)
            x = x_ref[rows, :]
            g = jnp.dot(x, wgb_ref[...], preferred_element_type=F32)
            u = jnp.dot(x, wub_ref[...], preferred_element_type=F32)
            o_ref[rows, :] = (_silu(g) * u).astype(o_ref.dtype)

        _for_each_row_block(n, ROW_SIZES_CHUNK, blk)

        def zero(r, carry):
            rows = pl.ds(pl.multiple_of(r * MOE_ALIGN, MOE_ALIGN), MOE_ALIGN)
            o_ref[rows, :] = jnp.zeros((MOE_ALIGN, o_ref.shape[1]), o_ref.dtype)
            return carry

        lax.fori_loop(n // MOE_ALIGN, o_ref.shape[0] // MOE_ALIGN, zero, 0)


def _moe_gmm1(xs, w_gu, layer, plan):
    rows, d = xs.shape
    f = w_gu.shape[3] // 2
    tn = _pick_tile(f, MOE_COLS, V7X_LANES)
    nj = f // tn
    w_max = plan["w_max"]

    def col(w, j, cn):
        return jnp.where(cn[w] > 0, j, nj - 1)

    grid_spec = pltpu.PrefetchScalarGridSpec(
        num_scalar_prefetch=3,
        grid=(w_max, nj),
        in_specs=[
            pl.BlockSpec((MOE_CHUNK, d), lambda w, j, ce, cb, cn: (cb[w], 0)),
            pl.BlockSpec((None, None, d, tn),
                         lambda w, j, ce, cb, cn: (layer, ce[w], 0, col(w, j, cn))),
            pl.BlockSpec((None, None, d, tn),
                         lambda w, j, ce, cb, cn: (layer, ce[w], 0, col(w, j, cn) + nj)),
        ],
        out_specs=pl.BlockSpec((MOE_CHUNK, tn), lambda w, j, ce, cb, cn: (cb[w], col(w, j, cn))),
        scratch_shapes=[pltpu.VMEM((d, tn), BF16), pltpu.VMEM((d, tn), BF16)],
    )
    return pl.pallas_call(
        _gmm1_kernel,
        out_shape=jax.ShapeDtypeStruct((rows, f), BF16),
        grid_spec=grid_spec,
        compiler_params=_params(("arbitrary", "arbitrary")),
        name="moe_gmm1",
    )(plan["chunk_e"], plan["chunk_blk"], plan["chunk_n"], xs, w_gu, w_gu)


def _gmm2_kernel(re_ref, rb_ref, rn_ref, h_ref, w_ref, o_ref, wb_ref):
    n = rn_ref[pl.program_id(0)]
    kk = pl.program_id(1)

    @pl.when(n > 0)
    def _():
        @pl.when(kk == 0)
        def _():
            o_ref[...] = jnp.zeros_like(o_ref)

        wb_ref[...] = w_ref[...].astype(BF16)

        def blk(off, size):
            rows = pl.ds(off, size)
            o_ref[rows, :] += jnp.dot(h_ref[rows, :], wb_ref[...],
                                      preferred_element_type=F32)

        _for_each_row_block(n, ROW_SIZES_ROWBLK, blk)


def _moe_gmm2(hs, w_down, layer, plan):
    rows, f = hs.shape
    d = w_down.shape[3]
    tk = _pick_tile(f, MOE_K, V7X_LANES)
    nk = f // tk
    r_max = plan["rb_blk"].shape[0]

    def kblk(r, kk, rn):
        return jnp.where(rn[r] > 0, kk, nk - 1)

    grid_spec = pltpu.PrefetchScalarGridSpec(
        num_scalar_prefetch=3,
        grid=(r_max, nk),
        in_specs=[
            pl.BlockSpec((MOE_ROWBLK, tk), lambda r, kk, re, rb, rn: (rb[r], kblk(r, kk, rn))),
            pl.BlockSpec((None, None, tk, d),
                         lambda r, kk, re, rb, rn: (layer, re[r], kblk(r, kk, rn), 0)),
        ],
        out_specs=pl.BlockSpec((MOE_ROWBLK, d), lambda r, kk, re, rb, rn: (rb[r], 0),
                               pipeline_mode=pl.Buffered(1)),
        scratch_shapes=[pltpu.VMEM((tk, d), BF16)],
    )
    return pl.pallas_call(
        _gmm2_kernel,
        out_shape=jax.ShapeDtypeStruct((rows, d), F32),
        grid_spec=grid_spec,
        compiler_params=_params(("arbitrary", "arbitrary")),
        name="moe_gmm2",
    )(plan["rb_e"], plan["rb_blk"], plan["rb_n"], hs, w_down)


def _combine_kernel(*refs, alpha, emit_bf16, head_tiles):
    first_ref, next_ref, x_ref, route_ref, ys_hbm, g_ref, b_ref, y_ref = refs[:8]
    yb_ref = refs[8] if emit_bf16 else None
    tail_ref = refs[8] if head_tiles is not None else None
    buf_ref, sem = refs[-2:]
    i = pl.program_id(0)
    n_rows = x_ref.shape[0]
    slot = i % 2

    def start(idx_ref, s):
        for k in range(2):
            _start_rows(ys_hbm, idx_ref, 2, k, buf_ref.at[s, k], n_rows, sem.at[s])

    @pl.when(i == 0)
    def _():
        start(first_ref, 0)

    @pl.when(i + 1 < pl.num_programs(0))
    def _():
        start(next_ref, 1 - slot)

    for k in range(2):
        _wait_rows(ys_hbm, buf_ref.at[slot, k], n_rows, sem.at[slot])
    w1 = route_ref[:, 2:3]
    w2 = route_ref[:, 3:4]
    z = alpha * x_ref[...] + (w1 * buf_ref[slot, 0] + w2 * buf_ref[slot, 1])
    y = _layer_norm_rows(z, g_ref[...], b_ref[...])
    if head_tiles is None:
        y_ref[...] = y
    else:
        @pl.when(i < head_tiles)
        def _():
            y_ref[...] = y

        @pl.when(i >= head_tiles)
        def _():
            tail_ref[...] = y
    if emit_bf16:
        yb_ref[...] = y.astype(BF16)


def _moe_combine(x, route, ys, plan, ln_g, ln_b, ln_layer, alpha, emit_bf16, head_rows):
    n, d = x.shape
    tq = _pick_tile(n, COMBINE_ROWS, V7X_BF16_ROWS)
    nt = n // tq
    slots = plan["slot"].reshape(nt, 1, 2 * tq)
    smem_blk = lambda index_map: pl.BlockSpec((1, 1, 2 * tq), index_map,
                                              memory_space=pltpu.SMEM)
    head_tiles = None
    if head_rows is not None:
        assert not emit_bf16 and head_rows % tq == 0 and 0 < head_rows < n
        head_tiles = head_rows // tq
        out_shape = [jax.ShapeDtypeStruct((head_rows, d), F32),
                     jax.ShapeDtypeStruct((n - head_rows, d), F32)]
        out_specs = [pl.BlockSpec((tq, d), lambda i: (jnp.minimum(i, head_tiles - 1), 0)),
                     pl.BlockSpec((tq, d), lambda i: (jnp.maximum(i - head_tiles, 0), 0))]
    else:
        out_shape = [jax.ShapeDtypeStruct((n, d), F32)]
        out_specs = [pl.BlockSpec((tq, d), lambda i: (i, 0))]
    if emit_bf16:
        out_shape.append(jax.ShapeDtypeStruct((n, d), BF16))
        out_specs.append(pl.BlockSpec((tq, d), lambda i: (i, 0)))
    return pl.pallas_call(
        functools.partial(_combine_kernel, alpha=alpha, emit_bf16=emit_bf16,
                          head_tiles=head_tiles),
        out_shape=out_shape,
        grid=(nt,),
        in_specs=[
            smem_blk(lambda i: (0, 0, 0)),
            smem_blk(lambda i: (jnp.minimum(i + 1, nt - 1), 0, 0)),
            pl.BlockSpec((tq, d), lambda i: (i, 0)),
            pl.BlockSpec((tq, V7X_LANES), lambda i: (i, 0)),
            pl.BlockSpec(memory_space=pl.ANY),
            pl.BlockSpec((None, 1, d), lambda i: (ln_layer, 0, 0)),
            pl.BlockSpec((None, 1, d), lambda i: (ln_layer, 0, 0)),
        ],
        out_specs=out_specs,
        scratch_shapes=[pltpu.VMEM((2, 2, tq, d), F32), pltpu.SemaphoreType.DMA((2,))],
        compiler_params=_params(("arbitrary",)),
        name="moe_combine",
    )(slots, slots, x, route, ys, _row_vec(ln_g), _row_vec(ln_b))


def _moe_ffn(x, route, w_gu, w_down, layer, ln_g, ln_b, ln_layer, alpha, emit_bf16,
             head_rows):
    n = x.shape[0]
    ids = route[:, :2].astype(jnp.int32)
    plan = _route_plan(ids[:, 0], ids[:, 1], n)
    xs = _moe_gather(x, plan)
    hs = _moe_gmm1(xs, w_gu, layer, plan)
    ys = _moe_gmm2(hs, w_down, layer, plan)
    return _moe_combine(x, route, ys, plan, ln_g, ln_b, ln_layer, alpha, emit_bf16,
                        head_rows)


def kernel(x_prompt, x_sample, state_conv, cache_k, cache_v, ln1_g, ln1_b, ln2_g, ln2_b,
           w_pw1, b_pw1, w_dw, b_dw, conv_norm_g, conv_norm_b, w_pw2, b_pw2,
           w_qkv, attn_sinks, w_o, w_ffn_gu, w_ffn_down, w_router, w_exp_gu, w_exp_down):
    batch, seq, d_model = x_prompt.shape
    dec_batch, dec_seq, _ = x_sample.shape
    depth = ln1_g.shape[0]
    alpha = float((2 * depth) ** 0.25)
    hd = d_model // N_HEADS
    kvw = N_KV_HEADS * hd
    n_p = batch * seq
    assert seq >= CONV_HIST and seq % WINDOW == 0

    x = jnp.concatenate([x_prompt.reshape(n_p, d_model),
                         x_sample.reshape(dec_batch * dec_seq, d_model)], axis=0)
    xb = x.astype(BF16)
    conv_p, conv_s, k_p, v_p, k_s, v_s = [], [], [], [], [], []

    def tail_rows(a, n_rows):
        return jnp.stack([a[(bi + 1) * seq - n_rows:(bi + 1) * seq] for bi in range(batch)])

    w_pw2, w_o, w_ffn_down = (w.astype(BF16) for w in (w_pw2, w_o, w_ffn_down))

    for i in range(depth):
        j = i // 2
        last = i == depth - 1
        if i % 2 == 0:
            u = _mm1_pair(xb, w_pw1, j, b_pw1, "glu", F32, f"pw1_glu_{i}")
            u_s = u[n_p:].reshape(dec_batch, dec_seq, u.shape[1])
            ext_s = jnp.concatenate([state_conv[j].astype(F32), u_s], axis=1)
            conv_p.append(tail_rows(u, CONV_HIST))
            conv_s.append(ext_s[:, -CONV_HIST:])
            c = _conv_prompt(u, batch, seq, j, w_dw, b_dw, conv_norm_g, conv_norm_b)
            c = _conv_sample(ext_s, c, n_p, j, w_dw, b_dw, conv_norm_g, conv_norm_b)
            x1, x1b = _mm2(c, w_pw2, j, b_pw2, x, ln1_g, ln1_b, i, alpha, None, True,
                           f"pw2_ln_{i}")
            hmid = _mm1_pair(x1b, w_ffn_gu, j, None, "swiglu", BF16, f"ffn_gu_{i}")
            x, *rest = _mm2(hmid, w_ffn_down, j, None, x1, ln2_g, ln2_b, i, alpha, None,
                            not last, f"ffn_down_ln_{i}")
            xb = rest[0] if rest else None
        else:
            qkv = _mm1_plain(xb, w_qkv, j, F32, f"qkv_{i}")
            kv_p = tail_rows(qkv[:, d_model:], WINDOW)
            kv_s = qkv[n_p:, d_model:].reshape(dec_batch, dec_seq, 2 * kvw)
            heads = lambda a: a.reshape(a.shape[0], a.shape[1], N_KV_HEADS, hd)
            w_buf = cache_k.shape[2]
            k_buf = cache_k[j].astype(F32)
            v_buf = cache_v[j].astype(F32)
            k_p.append(heads(kv_p[..., :kvw]))
            v_p.append(heads(kv_p[..., kvw:]))
            k_s.append(jnp.concatenate([k_buf, heads(kv_s[..., :kvw])], axis=1)[:, -w_buf:])
            v_s.append(jnp.concatenate([v_buf, heads(kv_s[..., kvw:])], axis=1)[:, -w_buf:])
            o = _attn_prompt(qkv, attn_sinks[j], batch, seq, d_model)
            o = _attn_sample(qkv, o, n_p, dec_batch, dec_seq, k_buf, v_buf, attn_sinks[j],
                             d_model)
            x1, route = _mm2(o, w_o, j, None, x, ln1_g, ln1_b, i, alpha, w_router, False,
                             f"wo_ln_router_{i}")
            tq = _pick_tile(x1.shape[0], COMBINE_ROWS, V7X_BF16_ROWS)
            split = n_p if last and n_p % tq == 0 else None
            x, *rest = _moe_ffn(x1, route, w_exp_gu, w_exp_down, j, ln2_g, ln2_b, i, alpha,
                                not last, split)
            if split is not None:
                x = (x, rest[0])
            else:
                xb = rest[0] if rest else None

    x_p, x_s = x if isinstance(x, tuple) else (x[:n_p], x[n_p:])
    y_p = x_p.reshape(batch, seq, d_model)
    y_s = x_s.reshape(dec_batch, dec_seq, d_model)
    return (y_p, y_s, jnp.stack(conv_p), jnp.stack(conv_s),
            jnp.stack(k_p), jnp.stack(v_p), jnp.stack(k_s), jnp.stack(v_s))
```

```python
import functools

import jax
import jax.numpy as jnp
from jax import lax
from jax.experimental import pallas as pl
from jax.experimental.pallas import tpu as pltpu

F32 = jnp.float32
BF16 = jnp.bfloat16

N_HEADS = 32
N_KV_HEADS = 4
GROUP = N_HEADS // N_KV_HEADS
WINDOW = 128
CONV_WIDTH = 31
CONV_HIST = CONV_WIDTH - 1
N_EXPERTS = 8
LN_EPS = 1e-5

V7X_LANES = 128
V7X_SUBLANES = 8
V7X_BF16_ROWS = 16
V7X_VMEM_LIMIT = 58 * 1024 * 1024

MM_ROWS_CAP = 1056
MM1_COLS = 512
MM2_K = 512
MM2_RESIDENT_BYTES = 9 * 1024 * 1024
MM2_RESIDENT_ROWS_CAP = 528
LN_ROWS_CAP = 264
MOE_CHUNK = 2560
MOE_ROWBLK = 2560
MOE_SUB = 256
MOE_ALIGN = 128
MOE_COLS = 512
MOE_K = 512
ROW_SIZES_CHUNK = (1024, 1024, 512, 256, 128)
ROW_SIZES_ROWBLK = ROW_SIZES_CHUNK
CONV_ROWS = 256
CONV_HALO = 32
CONV_TAIL = V7X_SUBLANES
CONV_COLS = 128
CONV_ROW_CHUNK = 128
COMBINE_ROWS = 256
DMA_UNROLL = 8
NEG_INF = float("-inf")


def _pick_tile(n, cap, mult):
    best = None
    for d in range(mult, min(n, cap) + 1, mult):
        if n % d == 0:
            best = d
    if best is None:
        raise ValueError(f"no tile for {n} (cap {cap}, multiple of {mult})")
    return best


def _params(semantics):
    return pltpu.CompilerParams(dimension_semantics=semantics,
                                vmem_limit_bytes=V7X_VMEM_LIMIT)


def _silu(x):
    return x * jax.nn.sigmoid(x)


def _layer_norm_rows(z, g, b):
    mu = jnp.mean(z, axis=-1, keepdims=True)
    d = z - mu
    var = jnp.mean(d * d, axis=-1, keepdims=True)
    return d * lax.rsqrt(var + LN_EPS) * g + b


def _mm1_pair_kernel(*refs, mode, has_bias):
    if has_bias:
        x_ref, wa_ref, wb_ref, ba_ref, bb_ref, o_ref = refs
    else:
        x_ref, wa_ref, wb_ref, o_ref = refs
    x = x_ref[...]
    a = jnp.dot(x, wa_ref[...].astype(BF16), preferred_element_type=F32)
    b = jnp.dot(x, wb_ref[...].astype(BF16), preferred_element_type=F32)
    if has_bias:
        a = a + ba_ref[...]
        b = b + bb_ref[...]
    if mode == "glu":
        r = a * jax.nn.sigmoid(b)
    else:
        r = _silu(a) * b
    o_ref[...] = r.astype(o_ref.dtype)


def _row_vec(stack):
    return stack.reshape(stack.shape[0], 1, stack.shape[1])


def _mm1_pair(x, w, layer, bias, mode, out_dtype, name):
    m, k = x.shape
    h = w.shape[2] // 2
    tm = _pick_tile(m, MM_ROWS_CAP, V7X_BF16_ROWS)
    tn = _pick_tile(h, MM1_COLS, V7X_LANES)
    nj = h // tn
    in_specs = [
        pl.BlockSpec((tm, k), lambda i, j: (i, 0)),
        pl.BlockSpec((None, k, tn), lambda i, j: (layer, 0, j)),
        pl.BlockSpec((None, k, tn), lambda i, j: (layer, 0, j + nj)),
    ]
    args = [x, w, w]
    if bias is not None:
        b2 = _row_vec(bias)
        in_specs += [pl.BlockSpec((None, 1, tn), lambda i, j: (layer, 0, j)),
                     pl.BlockSpec((None, 1, tn), lambda i, j: (layer, 0, j + nj))]
        args += [b2, b2]
    return pl.pallas_call(
        functools.partial(_mm1_pair_kernel, mode=mode, has_bias=bias is not None),
        out_shape=jax.ShapeDtypeStruct((m, h), out_dtype),
        grid=(m // tm, nj),
        in_specs=in_specs,
        out_specs=pl.BlockSpec((tm, tn), lambda i, j: (i, j)),
        compiler_params=_params(("parallel", "arbitrary")),
        name=name,
    )(*args)


def _mm1_plain_kernel(x_ref, w_ref, o_ref):
    o_ref[...] = jnp.dot(x_ref[...], w_ref[...].astype(BF16),
                         preferred_element_type=F32).astype(o_ref.dtype)


def _mm1_plain(x, w, layer, out_dtype, name):
    m, k = x.shape
    n = w.shape[2]
    tm = _pick_tile(m, MM_ROWS_CAP, V7X_BF16_ROWS)
    tn = _pick_tile(n, MM1_COLS, V7X_LANES)
    return pl.pallas_call(
        _mm1_plain_kernel,
        out_shape=jax.ShapeDtypeStruct((m, n), out_dtype),
        grid=(m // tm, n // tn),
        in_specs=[pl.BlockSpec((tm, k), lambda i, j: (i, 0)),
                  pl.BlockSpec((None, k, tn), lambda i, j: (layer, 0, j))],
        out_specs=pl.BlockSpec((tm, tn), lambda i, j: (i, j)),
        compiler_params=_params(("parallel", "arbitrary")),
        name=name,
    )(x, w)


def _route_from_logits(logits):
    lane = lax.broadcasted_iota(jnp.int32, logits.shape, 1).astype(F32)
    no_lane = float(V7X_LANES)
    lg = jnp.where(lane < N_EXPERTS, logits, NEG_INF)
    m1 = jnp.max(lg, axis=-1, keepdims=True)
    i1 = jnp.min(jnp.where(lg == m1, lane, no_lane), axis=-1, keepdims=True)
    lg2 = jnp.where(lane == i1, NEG_INF, lg)
    m2 = jnp.max(lg2, axis=-1, keepdims=True)
    i2 = jnp.min(jnp.where(lg2 == m2, lane, no_lane), axis=-1, keepdims=True)
    e2 = jnp.exp(m2 - m1)
    den = 1.0 + e2
    w1 = 1.0 / den
    w2 = e2 / den
    out = jnp.where(lane == 0.0, i1,
                    jnp.where(lane == 1.0, i2,
                              jnp.where(lane == 2.0, w1,
                                        jnp.where(lane == 3.0, w2, 0.0))))
    return out


def _mm2_kernel(*refs, alpha, has_bias, has_router, emit_bf16, row_half, ln_rows):
    refs = list(refs)
    h_ref, w_ref, r_ref, g_ref, b_ref = refs[:5]
    pos = 5
    bias_ref = wr_ref = yb_ref = route_ref = None
    if has_bias:
        bias_ref = refs[pos]
        pos += 1
    if has_router:
        wr_ref = refs[pos]
        pos += 1
    y_ref = refs[pos]
    pos += 1
    if emit_bf16:
        yb_ref = refs[pos]
        pos += 1
    if has_router:
        route_ref = refs[pos]

    kk = pl.program_id(1)
    tm = y_ref.shape[0]

    @pl.when(kk == 0)
    def _():
        z0 = alpha * r_ref[...]
        if has_bias:
            z0 = z0 + bias_ref[...]
        y_ref[...] = z0

    wb = w_ref[...].astype(BF16)
    for s in range(tm // row_half):
        rows = pl.ds(s * row_half, row_half)
        y_ref[rows, :] += jnp.dot(h_ref[rows, :], wb, preferred_element_type=F32)

    @pl.when(kk == pl.num_programs(1) - 1)
    def _():
        g = g_ref[...]
        b = b_ref[...]
        if has_router:
            wr = wr_ref[...].astype(BF16)

        def ln_chunk(c, carry):
            rows = pl.ds(pl.multiple_of(c * ln_rows, V7X_SUBLANES), ln_rows)
            y = _layer_norm_rows(y_ref[rows, :], g, b)
            y_ref[rows, :] = y
            if emit_bf16:
                yb_ref[rows, :] = y.astype(BF16)
            if has_router:
                logits = jnp.dot(y.astype(BF16), wr, preferred_element_type=F32)
                route_ref[rows, :] = _route_from_logits(logits)
            return carry

        lax.fori_loop(0, tm // ln_rows, ln_chunk, 0)


def _mm2(h, w, w_layer, bias, resid, ln_g, ln_b, ln_layer, alpha, w_router, emit_bf16, name):
    m, k = h.shape
    d = w.shape[2]
    once = pl.Buffered(1)
    if k * d * w.dtype.itemsize <= MM2_RESIDENT_BYTES:
        tm = _pick_tile(m, MM2_RESIDENT_ROWS_CAP, V7X_BF16_ROWS)
        tk, row_half = k, tm
        w_mode, resid_mode = once, None
    else:
        tm = _pick_tile(m, MM_ROWS_CAP, V7X_BF16_ROWS)
        tk = _pick_tile(k, MM2_K, V7X_LANES)
        row_half = tm // 2 if (tm // 2) % V7X_BF16_ROWS == 0 else tm
        w_mode, resid_mode = None, once
    ln_rows = _pick_tile(tm, LN_ROWS_CAP, V7X_SUBLANES)
    in_specs = [
        pl.BlockSpec((tm, tk), lambda i, kk: (i, kk)),
        pl.BlockSpec((None, tk, d), lambda i, kk: (w_layer, kk, 0), pipeline_mode=w_mode),
        pl.BlockSpec((tm, d), lambda i, kk: (i, 0), pipeline_mode=resid_mode),
        pl.BlockSpec((None, 1, d), lambda i, kk: (ln_layer, 0, 0)),
        pl.BlockSpec((None, 1, d), lambda i, kk: (ln_layer, 0, 0)),
    ]
    args = [h, w, resid, _row_vec(ln_g), _row_vec(ln_b)]
    if bias is not None:
        in_specs.append(pl.BlockSpec((None, 1, d), lambda i, kk: (w_layer, 0, 0)))
        args.append(_row_vec(bias))
    out_shape = [jax.ShapeDtypeStruct((m, d), F32)]
    out_specs = [pl.BlockSpec((tm, d), lambda i, kk: (i, 0))]
    if emit_bf16:
        out_shape.append(jax.ShapeDtypeStruct((m, d), BF16))
        out_specs.append(pl.BlockSpec((tm, d), lambda i, kk: (i, 0)))
    if w_router is not None:
        wr = jnp.pad(w_router, ((0, 0), (0, 0), (0, V7X_LANES - w_router.shape[2])))
        in_specs.append(pl.BlockSpec((None, d, V7X_LANES), lambda i, kk: (w_layer, 0, 0)))
        args.append(wr)
        out_shape.append(jax.ShapeDtypeStruct((m, V7X_LANES), F32))
        out_specs.append(pl.BlockSpec((tm, V7X_LANES), lambda i, kk: (i, 0)))
    return pl.pallas_call(
        functools.partial(_mm2_kernel, alpha=alpha, has_bias=bias is not None,
                          has_router=w_router is not None, emit_bf16=emit_bf16,
                          row_half=row_half, ln_rows=ln_rows),
        out_shape=out_shape,
        grid=(m // tm, k // tk),
        in_specs=in_specs,
        out_specs=out_specs,
        compiler_params=_params(("parallel", "arbitrary")),
        name=name,
    )(*args)


def _conv_ln_silu(ext_ref, pre_ref, w_ref, bdw_ref, g_ref, b_ref, o_ref, t_rows):
    c_total = pre_ref.shape[1]
    cols_blk = min(CONV_COLS, c_total)
    row_chunk = min(CONV_ROW_CHUNK, t_rows)
    lead = CONV_HALO - CONV_HIST
    n = row_chunk + V7X_SUBLANES

    def col_body(c, carry):
        cols = pl.ds(pl.multiple_of(c * cols_blk, V7X_LANES), cols_blk)
        bdw = bdw_ref[:, cols]
        for rc in range(t_rows // row_chunk):
            t0 = rc * row_chunk
            p = None
            for r in reversed(range(V7X_SUBLANES)):
                v = None
                for a in range((lead + CONV_WIDTH - 1 - r) // V7X_SUBLANES + 1):
                    k = V7X_SUBLANES * a + r - lead
                    if k < 0:
                        continue
                    term = w_ref[pl.ds(k, 1), cols] * ext_ref[
                        pl.ds(t0 + V7X_SUBLANES * a, n), cols]
                    v = term if v is None else v + term
                p = v if p is None else v + pltpu.roll(p, n - 1, axis=0)
            pre_ref[pl.ds(t0, row_chunk), cols] = p[:row_chunk] + bdw
        return carry

    lax.fori_loop(0, c_total // cols_blk, col_body, 0)
    y = _layer_norm_rows(pre_ref[...], g_ref[...], b_ref[...])
    o_ref[...] = _silu(y).astype(o_ref.dtype)


def _conv_prompt_kernel(prev_ref, cur_ref, w_ref, bdw_ref, g_ref, b_ref, o_ref,
                        ext_ref, pre_ref):
    i = pl.program_id(1)
    t_rows = cur_ref.shape[0]
    prev = prev_ref[...]
    ext_ref[pl.ds(0, CONV_HALO), :] = jnp.where(i == 0, jnp.zeros_like(prev), prev)
    ext_ref[pl.ds(CONV_HALO, t_rows), :] = cur_ref[...]
    ext_ref[pl.ds(CONV_HALO + t_rows, CONV_TAIL), :] = jnp.zeros(
        (CONV_TAIL, ext_ref.shape[1]), ext_ref.dtype)
    _conv_ln_silu(ext_ref, pre_ref, w_ref, bdw_ref, g_ref, b_ref, o_ref, t_rows)


def _conv_param_specs(c, layer):
    def fixed(*_):
        return (layer, 0, 0)
    return [pl.BlockSpec((None, CONV_WIDTH, c), fixed),
            pl.BlockSpec((None, 1, c), fixed),
            pl.BlockSpec((None, 1, c), fixed),
            pl.BlockSpec((None, 1, c), fixed)]


def _conv_prompt(u, batch, seq, layer, w_dw, b_dw, g, b):
    rows, c = u.shape
    t = _pick_tile(seq, CONV_ROWS, CONV_HALO)
    nt = seq // t
    halo_per_t = t // CONV_HALO

    def prev_map(bi, i):
        return (jnp.maximum(bi * (seq // CONV_HALO) + i * halo_per_t - 1, 0), 0)

    return pl.pallas_call(
        _conv_prompt_kernel,
        out_shape=jax.ShapeDtypeStruct((rows, c), BF16),
        grid=(batch, nt),
        in_specs=[
            pl.BlockSpec((CONV_HALO, c), prev_map),
            pl.BlockSpec((t, c), lambda bi, i: (bi * nt + i, 0)),
        ] + _conv_param_specs(c, layer),
        out_specs=pl.BlockSpec((t, c), lambda bi, i: (bi * nt + i, 0)),
        scratch_shapes=[pltpu.VMEM((t + CONV_HALO + CONV_TAIL, c), F32),
                        pltpu.VMEM((t, c), F32)],
        compiler_params=_params(("parallel", "arbitrary")),
        name="conv_prompt",
    )(u, u, w_dw, _row_vec(b_dw), _row_vec(g), _row_vec(b))


def _conv_sample_kernel(ext_ref, w_ref, bdw_ref, g_ref, b_ref, full_ref, o_ref, pre_ref):
    del full_ref
    _conv_ln_silu(ext_ref, pre_ref, w_ref, bdw_ref, g_ref, b_ref, o_ref, o_ref.shape[0])


def _conv_sample(ext, full, row0, layer, w_dw, b_dw, g, b):
    s, rows, c = ext.shape
    t = rows - CONV_HIST
    assert row0 % t == 0
    lead = CONV_HALO - CONV_HIST
    ext = jnp.pad(ext, ((0, 0), (lead, CONV_TAIL), (0, 0)))
    return pl.pallas_call(
        _conv_sample_kernel,
        out_shape=jax.ShapeDtypeStruct(full.shape, full.dtype),
        grid=(s,),
        in_specs=[pl.BlockSpec((None, rows + lead + CONV_TAIL, c), lambda i: (i, 0, 0))]
        + _conv_param_specs(c, layer) + [pl.BlockSpec(memory_space=pl.ANY)],
        out_specs=pl.BlockSpec((t, c), lambda i: (row0 // t + i, 0)),
        scratch_shapes=[pltpu.VMEM((t, c), F32)],
        input_output_aliases={5: 0},
        compiler_params=_params(("parallel",)),
        name="conv_sample",
    )(ext, w_dw, _row_vec(b_dw), _row_vec(g), _row_vec(b), full)


def _attend(q, k, v, mask, sink):
    s = lax.dot_general(q, k, (((1,), (1,)), ((), ())), preferred_element_type=F32)
    s = jnp.where(mask, s, NEG_INF)
    m = jnp.maximum(jnp.max(s, axis=-1, keepdims=True), sink)
    p = jnp.exp(s - m)
    den = jnp.sum(p, axis=-1, keepdims=True) + jnp.exp(sink - m)
    o = jnp.dot(p.astype(BF16), v, preferred_element_type=F32)
    return o / den


def _attn_prompt_kernel(sink_ref, q_ref, kp_ref, kc_ref, vp_ref, vc_ref, o_ref, *, hd):
    n = pl.program_id(1)
    blk = q_ref.shape[0]
    scale = hd ** -0.5
    rows = GROUP * blk
    band = (GROUP, blk, 2 * blk)
    qi = lax.broadcasted_iota(jnp.int32, band, 1).reshape(rows, 2 * blk)
    ci = lax.broadcasted_iota(jnp.int32, band, 2).reshape(rows, 2 * blk)
    gi = lax.broadcasted_iota(jnp.int32, (GROUP, blk, 1), 0).reshape(rows, 1)
    mask = (ci >= qi) & (ci <= qi + WINDOW) & ((ci >= blk) | (n > 0))
    kband = jnp.concatenate([kp_ref[...], kc_ref[...]], axis=0).astype(BF16)
    vband = jnp.concatenate([vp_ref[...], vc_ref[...]], axis=0).astype(BF16)
    for h in range(N_KV_HEADS):
        kh = kband[:, h * hd:(h + 1) * hd]
        vh = vband[:, h * hd:(h + 1) * hd]
        heads = [h * GROUP + g for g in range(GROUP)]
        q = jnp.concatenate([q_ref[:, hh * hd:(hh + 1) * hd] for hh in heads], axis=0)
        q = (q * scale).astype(BF16)
        sink = jnp.zeros((rows, 1), F32)
        for g, hh in enumerate(heads):
            sink = jnp.where(gi == g, sink_ref[hh], sink)
        o = _attend(q, kh, vh, mask, sink).astype(o_ref.dtype)
        for g, hh in enumerate(heads):
            o_ref[:, hh * hd:(hh + 1) * hd] = o[g * blk:(g + 1) * blk]


def _attn_prompt(qkv, sinks, batch, seq, d_model):
    hd = d_model // N_HEADS
    kvw = N_KV_HEADS * hd
    blk = WINDOW
    nb = seq // blk
    qcols = d_model // kvw

    def cur(off):
        return lambda bi, n, s: (bi * nb + n, off)

    def prev(off):
        return lambda bi, n, s: (bi * nb + jnp.maximum(n - 1, 0), off)

    grid_spec = pltpu.PrefetchScalarGridSpec(
        num_scalar_prefetch=1,
        grid=(batch, nb),
        in_specs=[
            pl.BlockSpec((blk, d_model), lambda bi, n, s: (bi * nb + n, 0)),
            pl.BlockSpec((blk, kvw), prev(qcols)),
            pl.BlockSpec((blk, kvw), cur(qcols)),
            pl.BlockSpec((blk, kvw), prev(qcols + 1)),
            pl.BlockSpec((blk, kvw), cur(qcols + 1)),
        ],
        out_specs=pl.BlockSpec((blk, d_model), lambda bi, n, s: (bi * nb + n, 0)),
    )
    return pl.pallas_call(
        functools.partial(_attn_prompt_kernel, hd=hd),
        out_shape=jax.ShapeDtypeStruct((qkv.shape[0], d_model), BF16),
        grid_spec=grid_spec,
        compiler_params=_params(("parallel", "arbitrary")),
        name="attn_prompt",
    )(sinks, qkv, qkv, qkv, qkv, qkv)


def _attn_sample_kernel(sink_ref, q_ref, kn_ref, vn_ref, kc_ref, vc_ref, full_ref, o_ref, *, hd):
    del full_ref
    scale = hd ** -0.5
    t_new = q_ref.shape[0]
    w_buf = kc_ref.shape[1]
    rows = GROUP * t_new
    s_len = w_buf + t_new
    grid3 = (GROUP, t_new, s_len)
    ti = lax.broadcasted_iota(jnp.int32, grid3, 1).reshape(rows, s_len)
    ci = lax.broadcasted_iota(jnp.int32, grid3, 2).reshape(rows, s_len)
    gi = lax.broadcasted_iota(jnp.int32, (GROUP, t_new, 1), 0).reshape(rows, 1)
    mask = (ci >= ti + (w_buf - WINDOW)) & (ci <= ti + w_buf)
    for h in range(N_KV_HEADS):
        cols = slice(h * hd, (h + 1) * hd)
        k = jnp.concatenate([kc_ref[h], kn_ref[:, cols]], axis=0).astype(BF16)
        v = jnp.concatenate([vc_ref[h], vn_ref[:, cols]], axis=0).astype(BF16)
        heads = [h * GROUP + g for g in range(GROUP)]
        q = jnp.concatenate([q_ref[:, hh * hd:(hh + 1) * hd] for hh in heads], axis=0)
        q = (q * scale).astype(BF16)
        sink = jnp.zeros((rows, 1), F32)
        for g, hh in enumerate(heads):
            sink = jnp.where(gi == g, sink_ref[hh], sink)
        o = _attend(q, k, v, mask, sink).astype(o_ref.dtype)
        for g, hh in enumerate(heads):
            o_ref[:, hh * hd:(hh + 1) * hd] = o[g * t_new:(g + 1) * t_new]


def _attn_sample(qkv, full, row0, n_seq, t_new, k_buf, v_buf, sinks, d_model):
    hd = d_model // N_HEADS
    kvw = N_KV_HEADS * hd
    w_buf = k_buf.shape[1]
    assert row0 % t_new == 0
    r0 = row0 // t_new
    qcols = d_model // kvw
    heads_first = lambda a: a.transpose(0, 2, 1, 3)
    cache = pl.BlockSpec((None, N_KV_HEADS, w_buf, hd), lambda i, sk: (i, 0, 0, 0))
    grid_spec = pltpu.PrefetchScalarGridSpec(
        num_scalar_prefetch=1,
        grid=(n_seq,),
        in_specs=[
            pl.BlockSpec((t_new, d_model), lambda i, sk: (r0 + i, 0)),
            pl.BlockSpec((t_new, kvw), lambda i, sk: (r0 + i, qcols)),
            pl.BlockSpec((t_new, kvw), lambda i, sk: (r0 + i, qcols + 1)),
            cache, cache,
            pl.BlockSpec(memory_space=pl.ANY),
        ],
        out_specs=pl.BlockSpec((t_new, d_model), lambda i, sk: (r0 + i, 0)),
    )
    return pl.pallas_call(
        functools.partial(_attn_sample_kernel, hd=hd),
        out_shape=jax.ShapeDtypeStruct(full.shape, full.dtype),
        grid_spec=grid_spec,
        input_output_aliases={6: 0},
        compiler_params=_params(("parallel",)),
        name="attn_sample",
    )(sinks, qkv, qkv, qkv, heads_first(k_buf), heads_first(v_buf), full)


def _compact(active, size):
    order = jnp.argsort(jnp.logical_not(active).astype(jnp.int32), stable=True)
    order = order[:size].astype(jnp.int32)
    return order, jnp.sum(active).astype(jnp.int32)


def _route_plan(i1, i2, n_tok):
    n_asg = 2 * n_tok
    halves = MOE_CHUNK // MOE_ROWBLK
    subs = MOE_CHUNK // MOE_SUB
    w_max = n_asg // MOE_CHUNK + N_EXPERTS
    r_max = n_asg // MOE_ROWBLK + N_EXPERTS
    s_max = n_asg // MOE_SUB + N_EXPERTS

    ei = jnp.stack([i1, i2], axis=1).reshape(-1)
    onehot = (ei[:, None] == jnp.arange(N_EXPERTS, dtype=jnp.int32)[None, :]).astype(jnp.int32)
    csum = jnp.cumsum(onehot, axis=0)
    counts = csum[-1]
    rank = jnp.take_along_axis(csum, ei[:, None], axis=1)[:, 0] - 1
    nch = (counts + MOE_CHUNK - 1) // MOE_CHUNK
    cend = jnp.cumsum(nch)
    cstart = cend - nch
    slot = (cstart[ei] * MOE_CHUNK + rank).astype(jnp.int32)
    n_chunks = cend[-1]

    wid = jnp.arange(w_max, dtype=jnp.int32)
    wcl = jnp.minimum(wid, n_chunks - 1)
    ce = jnp.sum((cend[None, :] <= wcl[:, None]).astype(jnp.int32), axis=1)
    ce = jnp.minimum(ce, N_EXPERTS - 1).astype(jnp.int32)
    crow = jnp.clip(counts[ce] - (wcl - cstart[ce]) * MOE_CHUNK, 0, MOE_CHUNK)
    crow = jnp.where(wid < n_chunks, crow, 0).astype(jnp.int32)

    def round_up(v, mlt):
        return ((v + mlt - 1) // mlt) * mlt

    rb_rows = jnp.clip(crow[:, None] - jnp.arange(halves, dtype=jnp.int32)[None, :] * MOE_ROWBLK,
                       0, MOE_ROWBLK).reshape(-1)
    rb_ids, n_rb = _compact(rb_rows > 0, r_max)
    rb_ids = rb_ids[jnp.minimum(jnp.arange(r_max), n_rb - 1)]
    rb_n = jnp.where(jnp.arange(r_max) < n_rb, round_up(rb_rows[rb_ids], MOE_ALIGN), 0)
    rb_e = ce[rb_ids // halves]

    sb_rows = jnp.clip(crow[:, None] - jnp.arange(subs, dtype=jnp.int32)[None, :] * MOE_SUB,
                       0, MOE_SUB).reshape(-1)
    sb_ids, n_sb = _compact(sb_rows > 0, s_max)
    sb_ids = sb_ids[jnp.minimum(jnp.arange(s_max), n_sb - 1)]
    src = jnp.zeros((w_max * MOE_CHUNK,), jnp.int32).at[slot].set(
        jnp.arange(n_asg, dtype=jnp.int32) // 2)
    src_sb = src.reshape(w_max * subs, MOE_SUB)[sb_ids]

    return dict(
        slot=slot, w_max=w_max,
        chunk_e=ce, chunk_blk=wcl.astype(jnp.int32),
        chunk_n=round_up(crow, MOE_ALIGN).astype(jnp.int32),
        rb_e=rb_e.astype(jnp.int32), rb_blk=rb_ids.astype(jnp.int32),
        rb_n=rb_n.astype(jnp.int32),
        sb_blk=sb_ids.astype(jnp.int32), n_sb=n_sb.reshape(1),
        src_sb=src_sb.reshape(s_max, 1, MOE_SUB).astype(jnp.int32),
    )


def _row_copy(src_hbm, row, dst_ref, dst_row, sem):
    return pltpu.make_async_copy(src_hbm.at[pl.ds(row, 1)], dst_ref.at[pl.ds(dst_row, 1)], sem)


def _start_rows(src_hbm, idx_ref, stride, offset, dst_ref, n_rows, sem):
    def body(c, carry):
        for u in range(DMA_UNROLL):
            r = c * DMA_UNROLL + u
            _row_copy(src_hbm, idx_ref[0, 0, stride * r + offset], dst_ref, r, sem).start()
        return carry

    lax.fori_loop(0, n_rows // DMA_UNROLL, body, 0)


def _wait_rows(src_hbm, dst_ref, n_rows, sem):
    def body(c, carry):
        for u in range(DMA_UNROLL):
            _row_copy(src_hbm, 0, dst_ref, c * DMA_UNROLL + u, sem).wait()
        return carry

    lax.fori_loop(0, n_rows // DMA_UNROLL, body, 0)


def _gather_kernel(sb_ref, nsb_ref, first_ref, next_ref, x_hbm, o_ref, buf_ref, sem):
    i = pl.program_id(0)
    n_sb = nsb_ref[0]
    n_rows = buf_ref.shape[1]
    slot = i % 2

    @pl.when((i == 0) & (n_sb > 0))
    def _():
        _start_rows(x_hbm, first_ref, 1, 0, buf_ref.at[0], n_rows, sem.at[0])

    @pl.when(i + 1 < n_sb)
    def _():
        _start_rows(x_hbm, next_ref, 1, 0, buf_ref.at[1 - slot], n_rows, sem.at[1 - slot])

    @pl.when(i < n_sb)
    def _():
        _wait_rows(x_hbm, buf_ref.at[slot], n_rows, sem.at[slot])
        o_ref[...] = buf_ref[slot].astype(o_ref.dtype)


def _moe_gather(x, plan):
    d = x.shape[1]
    s_max = plan["sb_blk"].shape[0]
    smem_blk = lambda index_map: pl.BlockSpec((1, 1, MOE_SUB), index_map,
                                              memory_space=pltpu.SMEM)
    grid_spec = pltpu.PrefetchScalarGridSpec(
        num_scalar_prefetch=2,
        grid=(s_max,),
        in_specs=[
            smem_blk(lambda i, sb, ns: (0, 0, 0)),
            smem_blk(lambda i, sb, ns: (jnp.minimum(i + 1, s_max - 1), 0, 0)),
            pl.BlockSpec(memory_space=pl.ANY),
        ],
        out_specs=pl.BlockSpec((MOE_SUB, d), lambda i, sb, ns: (sb[i], 0)),
        scratch_shapes=[pltpu.VMEM((2, MOE_SUB, d), F32), pltpu.SemaphoreType.DMA((2,))],
    )
    return pl.pallas_call(
        _gather_kernel,
        out_shape=jax.ShapeDtypeStruct((plan["w_max"] * MOE_CHUNK, d), BF16),
        grid_spec=grid_spec,
        compiler_params=_params(("arbitrary",)),
        name="moe_gather",
    )(plan["sb_blk"], plan["n_sb"], plan["src_sb"], plan["src_sb"], x)


def _for_each_row_block(n_rows, sizes, fn):
    off = jnp.int32(0)
    for size in sizes:
        take = (n_rows - off) >= size

        @pl.when(take)
        def _(off=off, size=size):
            fn(pl.multiple_of(off, MOE_ALIGN), size)

        off = off + jnp.where(take, size, 0)


def _gmm1_kernel(ce_ref, cb_ref, cn_ref, x_ref, wg_ref, wu_ref, o_ref, wgb_ref, wub_ref):
    n = cn_ref[pl.program_id(0)]

    @pl.when(n > 0)
    def _():
        wgb_ref[...] = wg_ref[...].astype(BF16)
        wub_ref[...] = wu_ref[...].astype(BF16)

        def blk(off, size):
            rows = pl.ds(off, size)
            x = x_ref[rows, :]
            g = jnp.dot(x, wgb_ref[...], preferred_element_type=F32)
            u = jnp.dot(x, wub_ref[...], preferred_element_type=F32)
            o_ref[rows, :] = (_silu(g) * u).astype(o_ref.dtype)

        _for_each_row_block(n, ROW_SIZES_CHUNK, blk)

        def zero(r, carry):
            rows = pl.ds(pl.multiple_of(r * MOE_ALIGN, MOE_ALIGN), MOE_ALIGN)
            o_ref[rows, :] = jnp.zeros((MOE_ALIGN, o_ref.shape[1]), o_ref.dtype)
            return carry

        lax.fori_loop(n // MOE_ALIGN, o_ref.shape[0] // MOE_ALIGN, zero, 0)


def _moe_gmm1(xs, w_gu, layer, plan):
    rows, d = xs.shape
    f = w_gu.shape[3] // 2
    tn = _pick_tile(f, MOE_COLS, V7X_LANES)
    nj = f // tn
    w_max = plan["w_max"]

    def col(w, j, cn):
        return jnp.where(cn[w] > 0, j, nj - 1)

    grid_spec = pltpu.PrefetchScalarGridSpec(
        num_scalar_prefetch=3,
        grid=(w_max, nj),
        in_specs=[
            pl.BlockSpec((MOE_CHUNK, d), lambda w, j, ce, cb, cn: (cb[w], 0)),
            pl.BlockSpec((None, None, d, tn),
                         lambda w, j, ce, cb, cn: (layer, ce[w], 0, col(w, j, cn))),
            pl.BlockSpec((None, None, d, tn),
                         lambda w, j, ce, cb, cn: (layer, ce[w], 0, col(w, j, cn) + nj)),
        ],
        out_specs=pl.BlockSpec((MOE_CHUNK, tn), lambda w, j, ce, cb, cn: (cb[w], col(w, j, cn))),
        scratch_shapes=[pltpu.VMEM((d, tn), BF16), pltpu.VMEM((d, tn), BF16)],
    )
    return pl.pallas_call(
        _gmm1_kernel,
        out_shape=jax.ShapeDtypeStruct((rows, f), BF16),
        grid_spec=grid_spec,
        compiler_params=_params(("arbitrary", "arbitrary")),
        name="moe_gmm1",
    )(plan["chunk_e"], plan["chunk_blk"], plan["chunk_n"], xs, w_gu, w_gu)


def _gmm2_kernel(re_ref, rb_ref, rn_ref, h_ref, w_ref, o_ref, wb_ref):
    n = rn_ref[pl.program_id(0)]
    kk = pl.program_id(1)

    @pl.when(n > 0)
    def _():
        @pl.when(kk == 0)
        def _():
            o_ref[...] = jnp.zeros_like(o_ref)

        wb_ref[...] = w_ref[...].astype(BF16)

        def blk(off, size):
            rows = pl.ds(off, size)
            o_ref[rows, :] += jnp.dot(h_ref[rows, :], wb_ref[...],
                                      preferred_element_type=F32)

        _for_each_row_block(n, ROW_SIZES_ROWBLK, blk)


def _moe_gmm2(hs, w_down, layer, plan):
    rows, f = hs.shape
    d = w_down.shape[3]
    tk = _pick_tile(f, MOE_K, V7X_LANES)
    nk = f // tk
    r_max = plan["rb_blk"].shape[0]

    def kblk(r, kk, rn):
        return jnp.where(rn[r] > 0, kk, nk - 1)

    grid_spec = pltpu.PrefetchScalarGridSpec(
        num_scalar_prefetch=3,
        grid=(r_max, nk),
        in_specs=[
            pl.BlockSpec((MOE_ROWBLK, tk), lambda r, kk, re, rb, rn: (rb[r], kblk(r, kk, rn))),
            pl.BlockSpec((None, None, tk, d),
                         lambda r, kk, re, rb, rn: (layer, re[r], kblk(r, kk, rn), 0)),
        ],
        out_specs=pl.BlockSpec((MOE_ROWBLK, d), lambda r, kk, re, rb, rn: (rb[r], 0),
                               pipeline_mode=pl.Buffered(1)),
        scratch_shapes=[pltpu.VMEM((tk, d), BF16)],
    )
    return pl.pallas_call(
        _gmm2_kernel,
        out_shape=jax.ShapeDtypeStruct((rows, d), F32),
        grid_spec=grid_spec,
        compiler_params=_params(("arbitrary", "arbitrary")),
        name="moe_gmm2",
    )(plan["rb_e"], plan["rb_blk"], plan["rb_n"], hs, w_down)


def _combine_kernel(*refs, alpha, emit_bf16, head_tiles):
    first_ref, next_ref, x_ref, route_ref, ys_hbm, g_ref, b_ref, y_ref = refs[:8]
    yb_ref = refs[8] if emit_bf16 else None
    tail_ref = refs[8] if head_tiles is not None else None
    buf_ref, sem = refs[-2:]
    i = pl.program_id(0)
    n_rows = x_ref.shape[0]
    slot = i % 2

    def start(idx_ref, s):
        for k in range(2):
            _start_rows(ys_hbm, idx_ref, 2, k, buf_ref.at[s, k], n_rows, sem.at[s])

    @pl.when(i == 0)
    def _():
        start(first_ref, 0)

    @pl.when(i + 1 < pl.num_programs(0))
    def _():
        start(next_ref, 1 - slot)

    for k in range(2):
        _wait_rows(ys_hbm, buf_ref.at[slot, k], n_rows, sem.at[slot])
    w1 = route_ref[:, 2:3]
    w2 = route_ref[:, 3:4]
    z = alpha * x_ref[...] + (w1 * buf_ref[slot, 0] + w2 * buf_ref[slot, 1])
    y = _layer_norm_rows(z, g_ref[...], b_ref[...])
    if head_tiles is None:
        y_ref[...] = y
    else:
        @pl.when(i < head_tiles)
        def _():
            y_ref[...] = y

        @pl.when(i >= head_tiles)
        def _():
            tail_ref[...] = y
    if emit_bf16:
        yb_ref[...] = y.astype(BF16)


def _moe_combine(x, route, ys, plan, ln_g, ln_b, ln_layer, alpha, emit_bf16, head_rows):
    n, d = x.shape
    tq = _pick_tile(n, COMBINE_ROWS, V7X_BF16_ROWS)
    nt = n // tq
    slots = plan["slot"].reshape(nt, 1, 2 * tq)
    smem_blk = lambda index_map: pl.BlockSpec((1, 1, 2 * tq), index_map,
                                              memory_space=pltpu.SMEM)
    head_tiles = None
    if head_rows is not None:
        assert not emit_bf16 and head_rows % tq == 0 and 0 < head_rows < n
        head_tiles = head_rows // tq
        out_shape = [jax.ShapeDtypeStruct((head_rows, d), F32),
                     jax.ShapeDtypeStruct((n - head_rows, d), F32)]
        out_specs = [pl.BlockSpec((tq, d), lambda i: (jnp.minimum(i, head_tiles - 1), 0)),
                     pl.BlockSpec((tq, d), lambda i: (jnp.maximum(i - head_tiles, 0), 0))]
    else:
        out_shape = [jax.ShapeDtypeStruct((n, d), F32)]
        out_specs = [pl.BlockSpec((tq, d), lambda i: (i, 0))]
    if emit_bf16:
        out_shape.append(jax.ShapeDtypeStruct((n, d), BF16))
        out_specs.append(pl.BlockSpec((tq, d), lambda i: (i, 0)))
    return pl.pallas_call(
        functools.partial(_combine_kernel, alpha=alpha, emit_bf16=emit_bf16,
                          head_tiles=head_tiles),
        out_shape=out_shape,
        grid=(nt,),
        in_specs=[
            smem_blk(lambda i: (0, 0, 0)),
            smem_blk(lambda i: (jnp.minimum(i + 1, nt - 1), 0, 0)),
            pl.BlockSpec((tq, d), lambda i: (i, 0)),
            pl.BlockSpec((tq, V7X_LANES), lambda i: (i, 0)),
            pl.BlockSpec(memory_space=pl.ANY),
            pl.BlockSpec((None, 1, d), lambda i: (ln_layer, 0, 0)),
            pl.BlockSpec((None, 1, d), lambda i: (ln_layer, 0, 0)),
        ],
        out_specs=out_specs,
        scratch_shapes=[pltpu.VMEM((2, 2, tq, d), F32), pltpu.SemaphoreType.DMA((2,))],
        compiler_params=_params(("arbitrary",)),
        name="moe_combine",
    )(slots, slots, x, route, ys, _row_vec(ln_g), _row_vec(ln_b))


def _moe_ffn(x, route, w_gu, w_down, layer, ln_g, ln_b, ln_layer, alpha, emit_bf16,
             head_rows):
    n = x.shape[0]
    ids = route[:, :2].astype(jnp.int32)
    plan = _route_plan(ids[:, 0], ids[:, 1], n)
    xs = _moe_gather(x, plan)
    hs = _moe_gmm1(xs, w_gu, layer, plan)
    ys = _moe_gmm2(hs, w_down, layer, plan)
    return _moe_combine(x, route, ys, plan, ln_g, ln_b, ln_layer, alpha, emit_bf16,
                        head_rows)


def kernel(x_prompt, x_sample, state_conv, cache_k, cache_v, ln1_g, ln1_b, ln2_g, ln2_b,
           w_pw1, b_pw1, w_dw, b_dw, conv_norm_g, conv_norm_b, w_pw2, b_pw2,
           w_qkv, attn_sinks, w_o, w_ffn_gu, w_ffn_down, w_router, w_exp_gu, w_exp_down):
    batch, seq, d_model = x_prompt.shape
    dec_batch, dec_seq, _ = x_sample.shape
    depth = ln1_g.shape[0]
    alpha = float((2 * depth) ** 0.25)
    hd = d_model // N_HEADS
    kvw = N_KV_HEADS * hd
    n_p = batch * seq
    assert seq >= CONV_HIST and seq % WINDOW == 0

    x = jnp.concatenate([x_prompt.reshape(n_p, d_model),
                         x_sample.reshape(dec_batch * dec_seq, d_model)], axis=0)
    xb = x.astype(BF16)
    conv_p, conv_s, k_p, v_p, k_s, v_s = [], [], [], [], [], []

    def tail_rows(a, n_rows):
        return jnp.stack([a[(bi + 1) * seq - n_rows:(bi + 1) * seq] for bi in range(batch)])

    w_pw2, w_o = w_pw2.astype(BF16), w_o.astype(BF16)

    for i in range(depth):
        j = i // 2
        last = i == depth - 1
        if i % 2 == 0:
            u = _mm1_pair(xb, w_pw1, j, b_pw1, "glu", F32, f"pw1_glu_{i}")
            u_s = u[n_p:].reshape(dec_batch, dec_seq, u.shape[1])
            ext_s = jnp.concatenate([state_conv[j].astype(F32), u_s], axis=1)
            conv_p.append(tail_rows(u, CONV_HIST))
            conv_s.append(ext_s[:, -CONV_HIST:])
            c = _conv_prompt(u, batch, seq, j, w_dw, b_dw, conv_norm_g, conv_norm_b)
            c = _conv_sample(ext_s, c, n_p, j, w_dw, b_dw, conv_norm_g, conv_norm_b)
            x1, x1b = _mm2(c, w_pw2, j, b_pw2, x, ln1_g, ln1_b, i, alpha, None, True,
                           f"pw2_ln_{i}")
            hmid = _mm1_pair(x1b, w_ffn_gu, j, None, "swiglu", BF16, f"ffn_gu_{i}")
            x, *rest = _mm2(hmid, w_ffn_down, j, None, x1, ln2_g, ln2_b, i, alpha, None,
                            not last, f"ffn_down_ln_{i}")
            xb = rest[0] if rest else None
        else:
            qkv = _mm1_plain(xb, w_qkv, j, F32, f"qkv_{i}")
            kv_p = tail_rows(qkv[:, d_model:], WINDOW)
            kv_s = qkv[n_p:, d_model:].reshape(dec_batch, dec_seq, 2 * kvw)
            heads = lambda a: a.reshape(a.shape[0], a.shape[1], N_KV_HEADS, hd)
            w_buf = cache_k.shape[2]
            k_buf = cache_k[j].astype(F32)
            v_buf = cache_v[j].astype(F32)
            k_p.append(heads(kv_p[..., :kvw]))
            v_p.append(heads(kv_p[..., kvw:]))
            k_s.append(jnp.concatenate([k_buf, heads(kv_s[..., :kvw])], axis=1)[:, -w_buf:])
            v_s.append(jnp.concatenate([v_buf, heads(kv_s[..., kvw:])], axis=1)[:, -w_buf:])
            o = _attn_prompt(qkv, attn_sinks[j], batch, seq, d_model)
            o = _attn_sample(qkv, o, n_p, dec_batch, dec_seq, k_buf, v_buf, attn_sinks[j],
                             d_model)
            x1, route = _mm2(o, w_o, j, None, x, ln1_g, ln1_b, i, alpha, w_router, False,
                             f"wo_ln_router_{i}")
            tq = _pick_tile(x1.shape[0], COMBINE_ROWS, V7X_BF16_ROWS)
            split = n_p if last and n_p % tq == 0 else None
            x, *rest = _moe_ffn(x1, route, w_exp_gu, w_exp_down, j, ln2_g, ln2_b, i, alpha,
                                not last, split)
            if split is not None:
                x = (x, rest[0])
            else:
                xb = rest[0] if rest else None

    x_p, x_s = x if isinstance(x, tuple) else (x[:n_p], x[n_p:])
    y_p = x_p.reshape(batch, seq, d_model)
    y_s = x_s.reshape(dec_batch, dec_seq, d_model)
    return (y_p, y_s, jnp.stack(conv_p), jnp.stack(conv_s),
            jnp.stack(k_p), jnp.stack(v_p), jnp.stack(k_s), jnp.stack(v_s))
```

```python
import functools

import jax
import jax.numpy as jnp
from jax import lax
from jax.experimental import pallas as pl
from jax.experimental.pallas import tpu as pltpu

F32 = jnp.float32
BF16 = jnp.bfloat16

N_HEADS = 32
N_KV_HEADS = 4
GROUP = N_HEADS // N_KV_HEADS
WINDOW = 128
CONV_WIDTH = 31
CONV_HIST = CONV_WIDTH - 1
N_EXPERTS = 8
LN_EPS = 1e-5

V7X_LANES = 128
V7X_SUBLANES = 8
V7X_BF16_ROWS = 16
V7X_VMEM_LIMIT = 58 * 1024 * 1024

MM_ROWS_CAP = 1056
MM1_COLS = 512
MM2_K = 512
MM2_RESIDENT_BYTES = 9 * 1024 * 1024
MM2_RESIDENT_ROWS_CAP = 528
LN_ROWS_CAP = 264
MOE_CHUNK = 2560
MOE_ROWBLK = 2560
MOE_SUB = 256
MOE_ALIGN = 128
MOE_COLS = 512
MOE_DOWN_K = 1024
MOE_DOWN_COLS = 1024
ROW_SIZES_CHUNK = (1024, 1024, 512, 256, 128)
ROW_SIZES_ROWBLK = ROW_SIZES_CHUNK
CONV_ROWS = 256
CONV_HALO = 32
CONV_TAIL = V7X_SUBLANES
CONV_COLS = 128
CONV_ROW_CHUNK = 128
COMBINE_ROWS = 256
DMA_UNROLL = 8
NEG_INF = float("-inf")


def _pick_tile(n, cap, mult):
    best = None
    for d in range(mult, min(n, cap) + 1, mult):
        if n % d == 0:
            best = d
    if best is None:
        raise ValueError(f"no tile for {n} (cap {cap}, multiple of {mult})")
    return best


def _params(semantics):
    return pltpu.CompilerParams(dimension_semantics=semantics,
                                vmem_limit_bytes=V7X_VMEM_LIMIT)


def _silu(x):
    return x * jax.nn.sigmoid(x)


def _layer_norm_rows(z, g, b):
    mu = jnp.mean(z, axis=-1, keepdims=True)
    d = z - mu
    var = jnp.mean(d * d, axis=-1, keepdims=True)
    return d * lax.rsqrt(var + LN_EPS) * g + b


def _mm1_pair_kernel(*refs, mode, has_bias):
    if has_bias:
        x_ref, wa_ref, wb_ref, ba_ref, bb_ref, o_ref = refs
    else:
        x_ref, wa_ref, wb_ref, o_ref = refs
    x = x_ref[...]
    a = jnp.dot(x, wa_ref[...].astype(BF16), preferred_element_type=F32)
    b = jnp.dot(x, wb_ref[...].astype(BF16), preferred_element_type=F32)
    if has_bias:
        a = a + ba_ref[...]
        b = b + bb_ref[...]
    if mode == "glu":
        r = a * jax.nn.sigmoid(b)
    else:
        r = _silu(a) * b
    o_ref[...] = r.astype(o_ref.dtype)


def _row_vec(stack):
    return stack.reshape(stack.shape[0], 1, stack.shape[1])


def _mm1_pair(x, w, layer, bias, mode, out_dtype, name):
    m, k = x.shape
    h = w.shape[2] // 2
    tm = _pick_tile(m, MM_ROWS_CAP, V7X_BF16_ROWS)
    tn = _pick_tile(h, MM1_COLS, V7X_LANES)
    nj = h // tn
    in_specs = [
        pl.BlockSpec((tm, k), lambda i, j: (i, 0)),
        pl.BlockSpec((None, k, tn), lambda i, j: (layer, 0, j)),
        pl.BlockSpec((None, k, tn), lambda i, j: (layer, 0, j + nj)),
    ]
    args = [x, w, w]
    if bias is not None:
        b2 = _row_vec(bias)
        in_specs += [pl.BlockSpec((None, 1, tn), lambda i, j: (layer, 0, j)),
                     pl.BlockSpec((None, 1, tn), lambda i, j: (layer, 0, j + nj))]
        args += [b2, b2]
    return pl.pallas_call(
        functools.partial(_mm1_pair_kernel, mode=mode, has_bias=bias is not None),
        out_shape=jax.ShapeDtypeStruct((m, h), out_dtype),
        grid=(m // tm, nj),
        in_specs=in_specs,
        out_specs=pl.BlockSpec((tm, tn), lambda i, j: (i, j)),
        compiler_params=_params(("parallel", "arbitrary")),
        name=name,
    )(*args)


def _mm1_plain_kernel(x_ref, w_ref, o_ref):
    o_ref[...] = jnp.dot(x_ref[...], w_ref[...].astype(BF16),
                         preferred_element_type=F32).astype(o_ref.dtype)


def _mm1_plain(x, w, layer, out_dtype, name):
    m, k = x.shape
    n = w.shape[2]
    tm = _pick_tile(m, MM_ROWS_CAP, V7X_BF16_ROWS)
    tn = _pick_tile(n, MM1_COLS, V7X_LANES)
    return pl.pallas_call(
        _mm1_plain_kernel,
        out_shape=jax.ShapeDtypeStruct((m, n), out_dtype),
        grid=(m // tm, n // tn),
        in_specs=[pl.BlockSpec((tm, k), lambda i, j: (i, 0)),
                  pl.BlockSpec((None, k, tn), lambda i, j: (layer, 0, j))],
        out_specs=pl.BlockSpec((tm, tn), lambda i, j: (i, j)),
        compiler_params=_params(("parallel", "arbitrary")),
        name=name,
    )(x, w)


def _route_from_logits(logits):
    lane = lax.broadcasted_iota(jnp.int32, logits.shape, 1).astype(F32)
    no_lane = float(V7X_LANES)
    lg = jnp.where(lane < N_EXPERTS, logits, NEG_INF)
    m1 = jnp.max(lg, axis=-1, keepdims=True)
    i1 = jnp.min(jnp.where(lg == m1, lane, no_lane), axis=-1, keepdims=True)
    lg2 = jnp.where(lane == i1, NEG_INF, lg)
    m2 = jnp.max(lg2, axis=-1, keepdims=True)
    i2 = jnp.min(jnp.where(lg2 == m2, lane, no_lane), axis=-1, keepdims=True)
    e2 = jnp.exp(m2 - m1)
    den = 1.0 + e2
    w1 = 1.0 / den
    w2 = e2 / den
    out = jnp.where(lane == 0.0, i1,
                    jnp.where(lane == 1.0, i2,
                              jnp.where(lane == 2.0, w1,
                                        jnp.where(lane == 3.0, w2, 0.0))))
    return out


def _mm2_kernel(*refs, alpha, has_bias, has_router, emit_bf16, row_half, ln_rows):
    refs = list(refs)
    h_ref, w_ref, r_ref, g_ref, b_ref = refs[:5]
    pos = 5
    bias_ref = wr_ref = yb_ref = route_ref = None
    if has_bias:
        bias_ref = refs[pos]
        pos += 1
    if has_router:
        wr_ref = refs[pos]
        pos += 1
    y_ref = refs[pos]
    pos += 1
    if emit_bf16:
        yb_ref = refs[pos]
        pos += 1
    if has_router:
        route_ref = refs[pos]

    kk = pl.program_id(1)
    tm = y_ref.shape[0]

    @pl.when(kk == 0)
    def _():
        z0 = alpha * r_ref[...]
        if has_bias:
            z0 = z0 + bias_ref[...]
        y_ref[...] = z0

    wb = w_ref[...].astype(BF16)
    for s in range(tm // row_half):
        rows = pl.ds(s * row_half, row_half)
        y_ref[rows, :] += jnp.dot(h_ref[rows, :], wb, preferred_element_type=F32)

    @pl.when(kk == pl.num_programs(1) - 1)
    def _():
        g = g_ref[...]
        b = b_ref[...]
        if has_router:
            wr = wr_ref[...].astype(BF16)

        def ln_chunk(c, carry):
            rows = pl.ds(pl.multiple_of(c * ln_rows, V7X_SUBLANES), ln_rows)
            y = _layer_norm_rows(y_ref[rows, :], g, b)
            y_ref[rows, :] = y
            if emit_bf16:
                yb_ref[rows, :] = y.astype(BF16)
            if has_router:
                logits = jnp.dot(y.astype(BF16), wr, preferred_element_type=F32)
                route_ref[rows, :] = _route_from_logits(logits)
            return carry

        lax.fori_loop(0, tm // ln_rows, ln_chunk, 0)


def _mm2(h, w, w_layer, bias, resid, ln_g, ln_b, ln_layer, alpha, w_router, emit_bf16, name):
    m, k = h.shape
    d = w.shape[2]
    once = pl.Buffered(1)
    if k * d * w.dtype.itemsize <= MM2_RESIDENT_BYTES:
        tm = _pick_tile(m, MM2_RESIDENT_ROWS_CAP, V7X_BF16_ROWS)
        tk, row_half = k, tm
        w_mode, resid_mode = once, None
    else:
        tm = _pick_tile(m, MM_ROWS_CAP, V7X_BF16_ROWS)
        tk = _pick_tile(k, MM2_K, V7X_LANES)
        row_half = tm // 2 if (tm // 2) % V7X_BF16_ROWS == 0 else tm
        w_mode, resid_mode = None, once
    ln_rows = _pick_tile(tm, LN_ROWS_CAP, V7X_SUBLANES)
    in_specs = [
        pl.BlockSpec((tm, tk), lambda i, kk: (i, kk)),
        pl.BlockSpec((None, tk, d), lambda i, kk: (w_layer, kk, 0), pipeline_mode=w_mode),
        pl.BlockSpec((tm, d), lambda i, kk: (i, 0), pipeline_mode=resid_mode),
        pl.BlockSpec((None, 1, d), lambda i, kk: (ln_layer, 0, 0)),
        pl.BlockSpec((None, 1, d), lambda i, kk: (ln_layer, 0, 0)),
    ]
    args = [h, w, resid, _row_vec(ln_g), _row_vec(ln_b)]
    if bias is not None:
        in_specs.append(pl.BlockSpec((None, 1, d), lambda i, kk: (w_layer, 0, 0)))
        args.append(_row_vec(bias))
    out_shape = [jax.ShapeDtypeStruct((m, d), F32)]
    out_specs = [pl.BlockSpec((tm, d), lambda i, kk: (i, 0))]
    if emit_bf16:
        out_shape.append(jax.ShapeDtypeStruct((m, d), BF16))
        out_specs.append(pl.BlockSpec((tm, d), lambda i, kk: (i, 0)))
    if w_router is not None:
        wr = jnp.pad(w_router, ((0, 0), (0, 0), (0, V7X_LANES - w_router.shape[2])))
        in_specs.append(pl.BlockSpec((None, d, V7X_LANES), lambda i, kk: (w_layer, 0, 0)))
        args.append(wr)
        out_shape.append(jax.ShapeDtypeStruct((m, V7X_LANES), F32))
        out_specs.append(pl.BlockSpec((tm, V7X_LANES), lambda i, kk: (i, 0)))
    return pl.pallas_call(
        functools.partial(_mm2_kernel, alpha=alpha, has_bias=bias is not None,
                          has_router=w_router is not None, emit_bf16=emit_bf16,
                          row_half=row_half, ln_rows=ln_rows),
        out_shape=out_shape,
        grid=(m // tm, k // tk),
        in_specs=in_specs,
        out_specs=out_specs,
        compiler_params=_params(("parallel", "arbitrary")),
        name=name,
    )(*args)


def _conv_ln_silu(ext_ref, pre_ref, w_ref, bdw_ref, g_ref, b_ref, o_ref, t_rows):
    c_total = pre_ref.shape[1]
    cols_blk = min(CONV_COLS, c_total)
    row_chunk = min(CONV_ROW_CHUNK, t_rows)
    lead = CONV_HALO - CONV_HIST
    n = row_chunk + V7X_SUBLANES

    def col_body(c, carry):
        cols = pl.ds(pl.multiple_of(c * cols_blk, V7X_LANES), cols_blk)
        bdw = bdw_ref[:, cols]
        for rc in range(t_rows // row_chunk):
            t0 = rc * row_chunk
            p = None
            for r in reversed(range(V7X_SUBLANES)):
                v = None
                for a in range((lead + CONV_WIDTH - 1 - r) // V7X_SUBLANES + 1):
                    k = V7X_SUBLANES * a + r - lead
                    if k < 0:
                        continue
                    term = w_ref[pl.ds(k, 1), cols] * ext_ref[
                        pl.ds(t0 + V7X_SUBLANES * a, n), cols]
                    v = term if v is None else v + term
                p = v if p is None else v + pltpu.roll(p, n - 1, axis=0)
            pre_ref[pl.ds(t0, row_chunk), cols] = p[:row_chunk] + bdw
        return carry

    lax.fori_loop(0, c_total // cols_blk, col_body, 0)
    y = _layer_norm_rows(pre_ref[...], g_ref[...], b_ref[...])
    o_ref[...] = _silu(y).astype(o_ref.dtype)


def _conv_prompt_kernel(prev_ref, cur_ref, w_ref, bdw_ref, g_ref, b_ref, o_ref,
                        ext_ref, pre_ref):
    i = pl.program_id(1)
    t_rows = cur_ref.shape[0]
    prev = prev_ref[...]
    ext_ref[pl.ds(0, CONV_HALO), :] = jnp.where(i == 0, jnp.zeros_like(prev), prev)
    ext_ref[pl.ds(CONV_HALO, t_rows), :] = cur_ref[...]
    ext_ref[pl.ds(CONV_HALO + t_rows, CONV_TAIL), :] = jnp.zeros(
        (CONV_TAIL, ext_ref.shape[1]), ext_ref.dtype)
    _conv_ln_silu(ext_ref, pre_ref, w_ref, bdw_ref, g_ref, b_ref, o_ref, t_rows)


def _conv_param_specs(c, layer):
    def fixed(*_):
        return (layer, 0, 0)
    return [pl.BlockSpec((None, CONV_WIDTH, c), fixed),
            pl.BlockSpec((None, 1, c), fixed),
            pl.BlockSpec((None, 1, c), fixed),
            pl.BlockSpec((None, 1, c), fixed)]


def _conv_prompt(u, batch, seq, layer, w_dw, b_dw, g, b):
    rows, c = u.shape
    t = _pick_tile(seq, CONV_ROWS, CONV_HALO)
    nt = seq // t
    halo_per_t = t // CONV_HALO

    def prev_map(bi, i):
        return (jnp.maximum(bi * (seq // CONV_HALO) + i * halo_per_t - 1, 0), 0)

    return pl.pallas_call(
        _conv_prompt_kernel,
        out_shape=jax.ShapeDtypeStruct((rows, c), BF16),
        grid=(batch, nt),
        in_specs=[
            pl.BlockSpec((CONV_HALO, c), prev_map),
            pl.BlockSpec((t, c), lambda bi, i: (bi * nt + i, 0)),
        ] + _conv_param_specs(c, layer),
        out_specs=pl.BlockSpec((t, c), lambda bi, i: (bi * nt + i, 0)),
        scratch_shapes=[pltpu.VMEM((t + CONV_HALO + CONV_TAIL, c), F32),
                        pltpu.VMEM((t, c), F32)],
        compiler_params=_params(("parallel", "arbitrary")),
        name="conv_prompt",
    )(u, u, w_dw, _row_vec(b_dw), _row_vec(g), _row_vec(b))


def _conv_sample_kernel(ext_ref, w_ref, bdw_ref, g_ref, b_ref, full_ref, o_ref, pre_ref):
    del full_ref
    _conv_ln_silu(ext_ref, pre_ref, w_ref, bdw_ref, g_ref, b_ref, o_ref, o_ref.shape[0])


def _conv_sample(ext, full, row0, layer, w_dw, b_dw, g, b):
    s, rows, c = ext.shape
    t = rows - CONV_HIST
    assert row0 % t == 0
    lead = CONV_HALO - CONV_HIST
    ext = jnp.pad(ext, ((0, 0), (lead, CONV_TAIL), (0, 0)))
    return pl.pallas_call(
        _conv_sample_kernel,
        out_shape=jax.ShapeDtypeStruct(full.shape, full.dtype),
        grid=(s,),
        in_specs=[pl.BlockSpec((None, rows + lead + CONV_TAIL, c), lambda i: (i, 0, 0))]
        + _conv_param_specs(c, layer) + [pl.BlockSpec(memory_space=pl.ANY)],
        out_specs=pl.BlockSpec((t, c), lambda i: (row0 // t + i, 0)),
        scratch_shapes=[pltpu.VMEM((t, c), F32)],
        input_output_aliases={5: 0},
        compiler_params=_params(("parallel",)),
        name="conv_sample",
    )(ext, w_dw, _row_vec(b_dw), _row_vec(g), _row_vec(b), full)


def _attend(q, k, v, mask, sink):
    s = lax.dot_general(q, k, (((1,), (1,)), ((), ())), preferred_element_type=F32)
    s = jnp.where(mask, s, NEG_INF)
    m = jnp.maximum(jnp.max(s, axis=-1, keepdims=True), sink)
    p = jnp.exp(s - m)
    den = jnp.sum(p, axis=-1, keepdims=True) + jnp.exp(sink - m)
    o = jnp.dot(p.astype(BF16), v, preferred_element_type=F32)
    return o / den


def _attn_prompt_kernel(sink_ref, q_ref, kp_ref, kc_ref, vp_ref, vc_ref, o_ref, *, hd):
    n = pl.program_id(1)
    blk = q_ref.shape[0]
    scale = hd ** -0.5
    rows = GROUP * blk
    band = (GROUP, blk, 2 * blk)
    qi = lax.broadcasted_iota(jnp.int32, band, 1).reshape(rows, 2 * blk)
    ci = lax.broadcasted_iota(jnp.int32, band, 2).reshape(rows, 2 * blk)
    gi = lax.broadcasted_iota(jnp.int32, (GROUP, blk, 1), 0).reshape(rows, 1)
    mask = (ci >= qi) & (ci <= qi + WINDOW) & ((ci >= blk) | (n > 0))
    kband = jnp.concatenate([kp_ref[...], kc_ref[...]], axis=0).astype(BF16)
    vband = jnp.concatenate([vp_ref[...], vc_ref[...]], axis=0).astype(BF16)
    for h in range(N_KV_HEADS):
        kh = kband[:, h * hd:(h + 1) * hd]
        vh = vband[:, h * hd:(h + 1) * hd]
        heads = [h * GROUP + g for g in range(GROUP)]
        q = jnp.concatenate([q_ref[:, hh * hd:(hh + 1) * hd] for hh in heads], axis=0)
        q = (q * scale).astype(BF16)
        sink = jnp.zeros((rows, 1), F32)
        for g, hh in enumerate(heads):
            sink = jnp.where(gi == g, sink_ref[hh], sink)
        o = _attend(q, kh, vh, mask, sink).astype(o_ref.dtype)
        for g, hh in enumerate(heads):
            o_ref[:, hh * hd:(hh + 1) * hd] = o[g * blk:(g + 1) * blk]


def _attn_prompt(qkv, sinks, batch, seq, d_model):
    hd = d_model // N_HEADS
    kvw = N_KV_HEADS * hd
    blk = WINDOW
    nb = seq // blk
    qcols = d_model // kvw

    def cur(off):
        return lambda bi, n, s: (bi * nb + n, off)

    def prev(off):
        return lambda bi, n, s: (bi * nb + jnp.maximum(n - 1, 0), off)

    grid_spec = pltpu.PrefetchScalarGridSpec(
        num_scalar_prefetch=1,
        grid=(batch, nb),
        in_specs=[
            pl.BlockSpec((blk, d_model), lambda bi, n, s: (bi * nb + n, 0)),
            pl.BlockSpec((blk, kvw), prev(qcols)),
            pl.BlockSpec((blk, kvw), cur(qcols)),
            pl.BlockSpec((blk, kvw), prev(qcols + 1)),
            pl.BlockSpec((blk, kvw), cur(qcols + 1)),
        ],
        out_specs=pl.BlockSpec((blk, d_model), lambda bi, n, s: (bi * nb + n, 0)),
    )
    return pl.pallas_call(
        functools.partial(_attn_prompt_kernel, hd=hd),
        out_shape=jax.ShapeDtypeStruct((qkv.shape[0], d_model), BF16),
        grid_spec=grid_spec,
        compiler_params=_params(("parallel", "arbitrary")),
        name="attn_prompt",
    )(sinks, qkv, qkv, qkv, qkv, qkv)


def _attn_sample_kernel(sink_ref, q_ref, kn_ref, vn_ref, kc_ref, vc_ref, full_ref, o_ref, *, hd):
    del full_ref
    scale = hd ** -0.5
    t_new = q_ref.shape[0]
    w_buf = kc_ref.shape[1]
    rows = GROUP * t_new
    s_len = w_buf + t_new
    grid3 = (GROUP, t_new, s_len)
    ti = lax.broadcasted_iota(jnp.int32, grid3, 1).reshape(rows, s_len)
    ci = lax.broadcasted_iota(jnp.int32, grid3, 2).reshape(rows, s_len)
    gi = lax.broadcasted_iota(jnp.int32, (GROUP, t_new, 1), 0).reshape(rows, 1)
    mask = (ci >= ti + (w_buf - WINDOW)) & (ci <= ti + w_buf)
    for h in range(N_KV_HEADS):
        cols = slice(h * hd, (h + 1) * hd)
        k = jnp.concatenate([kc_ref[h], kn_ref[:, cols]], axis=0).astype(BF16)
        v = jnp.concatenate([vc_ref[h], vn_ref[:, cols]], axis=0).astype(BF16)
        heads = [h * GROUP + g for g in range(GROUP)]
        q = jnp.concatenate([q_ref[:, hh * hd:(hh + 1) * hd] for hh in heads], axis=0)
        q = (q * scale).astype(BF16)
        sink = jnp.zeros((rows, 1), F32)
        for g, hh in enumerate(heads):
            sink = jnp.where(gi == g, sink_ref[hh], sink)
        o = _attend(q, k, v, mask, sink).astype(o_ref.dtype)
        for g, hh in enumerate(heads):
            o_ref[:, hh * hd:(hh + 1) * hd] = o[g * t_new:(g + 1) * t_new]


def _attn_sample(qkv, full, row0, n_seq, t_new, k_buf, v_buf, sinks, d_model):
    hd = d_model // N_HEADS
    kvw = N_KV_HEADS * hd
    w_buf = k_buf.shape[1]
    assert row0 % t_new == 0
    r0 = row0 // t_new
    qcols = d_model // kvw
    heads_first = lambda a: a.transpose(0, 2, 1, 3)
    cache = pl.BlockSpec((None, N_KV_HEADS, w_buf, hd), lambda i, sk: (i, 0, 0, 0))
    grid_spec = pltpu.PrefetchScalarGridSpec(
        num_scalar_prefetch=1,
        grid=(n_seq,),
        in_specs=[
            pl.BlockSpec((t_new, d_model), lambda i, sk: (r0 + i, 0)),
            pl.BlockSpec((t_new, kvw), lambda i, sk: (r0 + i, qcols)),
            pl.BlockSpec((t_new, kvw), lambda i, sk: (r0 + i, qcols + 1)),
            cache, cache,
            pl.BlockSpec(memory_space=pl.ANY),
        ],
        out_specs=pl.BlockSpec((t_new, d_model), lambda i, sk: (r0 + i, 0)),
    )
    return pl.pallas_call(
        functools.partial(_attn_sample_kernel, hd=hd),
        out_shape=jax.ShapeDtypeStruct(full.shape, full.dtype),
        grid_spec=grid_spec,
        input_output_aliases={6: 0},
        compiler_params=_params(("parallel",)),
        name="attn_sample",
    )(sinks, qkv, qkv, qkv, heads_first(k_buf), heads_first(v_buf), full)


def _compact(active, size):
    order = jnp.argsort(jnp.logical_not(active).astype(jnp.int32), stable=True)
    order = order[:size].astype(jnp.int32)
    return order, jnp.sum(active).astype(jnp.int32)


def _route_plan(i1, i2, n_tok):
    n_asg = 2 * n_tok
    halves = MOE_CHUNK // MOE_ROWBLK
    subs = MOE_CHUNK // MOE_SUB
    w_max = n_asg // MOE_CHUNK + N_EXPERTS
    r_max = n_asg // MOE_ROWBLK + N_EXPERTS
    s_max = n_asg // MOE_SUB + N_EXPERTS

    ei = jnp.stack([i1, i2], axis=1).reshape(-1)
    onehot = (ei[:, None] == jnp.arange(N_EXPERTS, dtype=jnp.int32)[None, :]).astype(jnp.int32)
    csum = jnp.cumsum(onehot, axis=0)
    counts = csum[-1]
    rank = jnp.take_along_axis(csum, ei[:, None], axis=1)[:, 0] - 1
    nch = (counts + MOE_CHUNK - 1) // MOE_CHUNK
    cend = jnp.cumsum(nch)
    cstart = cend - nch
    slot = (cstart[ei] * MOE_CHUNK + rank).astype(jnp.int32)
    n_chunks = cend[-1]

    wid = jnp.arange(w_max, dtype=jnp.int32)
    wcl = jnp.minimum(wid, n_chunks - 1)
    ce = jnp.sum((cend[None, :] <= wcl[:, None]).astype(jnp.int32), axis=1)
    ce = jnp.minimum(ce, N_EXPERTS - 1).astype(jnp.int32)
    crow = jnp.clip(counts[ce] - (wcl - cstart[ce]) * MOE_CHUNK, 0, MOE_CHUNK)
    crow = jnp.where(wid < n_chunks, crow, 0).astype(jnp.int32)

    def round_up(v, mlt):
        return ((v + mlt - 1) // mlt) * mlt

    rb_rows = jnp.clip(crow[:, None] - jnp.arange(halves, dtype=jnp.int32)[None, :] * MOE_ROWBLK,
                       0, MOE_ROWBLK).reshape(-1)
    rb_ids, n_rb = _compact(rb_rows > 0, r_max)
    rb_ids = rb_ids[jnp.minimum(jnp.arange(r_max), n_rb - 1)]
    rb_n = jnp.where(jnp.arange(r_max) < n_rb, round_up(rb_rows[rb_ids], MOE_ALIGN), 0)
    rb_e = ce[rb_ids // halves]

    sb_rows = jnp.clip(crow[:, None] - jnp.arange(subs, dtype=jnp.int32)[None, :] * MOE_SUB,
                       0, MOE_SUB).reshape(-1)
    sb_ids, n_sb = _compact(sb_rows > 0, s_max)
    sb_ids = sb_ids[jnp.minimum(jnp.arange(s_max), n_sb - 1)]
    src = jnp.zeros((w_max * MOE_CHUNK,), jnp.int32).at[slot].set(
        jnp.arange(n_asg, dtype=jnp.int32) // 2)
    src_sb = src.reshape(w_max * subs, MOE_SUB)[sb_ids]

    return dict(
        slot=slot, w_max=w_max,
        chunk_e=ce, chunk_blk=wcl.astype(jnp.int32),
        chunk_n=round_up(crow, MOE_ALIGN).astype(jnp.int32),
        rb_e=rb_e.astype(jnp.int32), rb_blk=rb_ids.astype(jnp.int32),
        rb_n=rb_n.astype(jnp.int32),
        sb_blk=sb_ids.astype(jnp.int32), n_sb=n_sb.reshape(1),
        src_sb=src_sb.reshape(s_max, 1, MOE_SUB).astype(jnp.int32),
    )


def _row_copy(src_hbm, row, dst_ref, dst_row, sem):
    return pltpu.make_async_copy(src_hbm.at[pl.ds(row, 1)], dst_ref.at[pl.ds(dst_row, 1)], sem)


def _start_rows(src_hbm, idx_ref, stride, offset, dst_ref, n_rows, sem):
    def body(c, carry):
        for u in range(DMA_UNROLL):
            r = c * DMA_UNROLL + u
            _row_copy(src_hbm, idx_ref[0, 0, stride * r + offset], dst_ref, r, sem).start()
        return carry

    lax.fori_loop(0, n_rows // DMA_UNROLL, body, 0)


def _wait_rows(src_hbm, dst_ref, n_rows, sem):
    def body(c, carry):
        for u in range(DMA_UNROLL):
            _row_copy(src_hbm, 0, dst_ref, c * DMA_UNROLL + u, sem).wait()
        return carry

    lax.fori_loop(0, n_rows // DMA_UNROLL, body, 0)


def _gather_kernel(sb_ref, nsb_ref, first_ref, next_ref, x_hbm, o_ref, buf_ref, sem):
    i = pl.program_id(0)
    n_sb = nsb_ref[0]
    n_rows = buf_ref.shape[1]
    slot = i % 2

    @pl.when((i == 0) & (n_sb > 0))
    def _():
        _start_rows(x_hbm, first_ref, 1, 0, buf_ref.at[0], n_rows, sem.at[0])

    @pl.when(i + 1 < n_sb)
    def _():
        _start_rows(x_hbm, next_ref, 1, 0, buf_ref.at[1 - slot], n_rows, sem.at[1 - slot])

    @pl.when(i < n_sb)
    def _():
        _wait_rows(x_hbm, buf_ref.at[slot], n_rows, sem.at[slot])
        o_ref[...] = buf_ref[slot].astype(o_ref.dtype)


def _moe_gather(x, plan):
    d = x.shape[1]
    s_max = plan["sb_blk"].shape[0]
    smem_blk = lambda index_map: pl.BlockSpec((1, 1, MOE_SUB), index_map,
                                              memory_space=pltpu.SMEM)
    grid_spec = pltpu.PrefetchScalarGridSpec(
        num_scalar_prefetch=2,
        grid=(s_max,),
        in_specs=[
            smem_blk(lambda i, sb, ns: (0, 0, 0)),
            smem_blk(lambda i, sb, ns: (jnp.minimum(i + 1, s_max - 1), 0, 0)),
            pl.BlockSpec(memory_space=pl.ANY),
        ],
        out_specs=pl.BlockSpec((MOE_SUB, d), lambda i, sb, ns: (sb[i], 0)),
        scratch_shapes=[pltpu.VMEM((2, MOE_SUB, d), F32), pltpu.SemaphoreType.DMA((2,))],
    )
    return pl.pallas_call(
        _gather_kernel,
        out_shape=jax.ShapeDtypeStruct((plan["w_max"] * MOE_CHUNK, d), BF16),
        grid_spec=grid_spec,
        compiler_params=_params(("arbitrary",)),
        name="moe_gather",
    )(plan["sb_blk"], plan["n_sb"], plan["src_sb"], plan["src_sb"], x)


def _for_each_row_block(n_rows, sizes, fn):
    off = jnp.int32(0)
    for size in sizes:
        take = (n_rows - off) >= size

        @pl.when(take)
        def _(off=off, size=size):
            fn(pl.multiple_of(off, MOE_ALIGN), size)

        off = off + jnp.where(take, size, 0)


def _gmm1_kernel(ce_ref, cb_ref, cn_ref, x_ref, wg_ref, wu_ref, o_ref, wgb_ref, wub_ref):
    n = cn_ref[pl.program_id(0)]

    @pl.when(n > 0)
    def _():
        wgb_ref[...] = wg_ref[...].astype(BF16)
        wub_ref[...] = wu_ref[...].astype(BF16)

        def blk(off, size):
            rows = pl.ds(off, size)
            x = x_ref[rows, :]
            g = jnp.dot(x, wgb_ref[...], preferred_element_type=F32)
            u = jnp.dot(x, wub_ref[...], preferred_element_type=F32)
            o_ref[rows, :] = (_silu(g) * u).astype(o_ref.dtype)

        _for_each_row_block(n, ROW_SIZES_CHUNK, blk)

        def zero(r, carry):
            rows = pl.ds(pl.multiple_of(r * MOE_ALIGN, MOE_ALIGN), MOE_ALIGN)
            o_ref[rows, :] = jnp.zeros((MOE_ALIGN, o_ref.shape[1]), o_ref.dtype)
            return carry

        lax.fori_loop(n // MOE_ALIGN, o_ref.shape[0] // MOE_ALIGN, zero, 0)


def _moe_gmm1(xs, w_gu, layer, plan):
    rows, d = xs.shape
    f = w_gu.shape[3] // 2
    tn = _pick_tile(f, MOE_COLS, V7X_LANES)
    nj = f // tn
    w_max = plan["w_max"]

    def col(w, j, cn):
        return jnp.where(cn[w] > 0, j, nj - 1)

    grid_spec = pltpu.PrefetchScalarGridSpec(
        num_scalar_prefetch=3,
        grid=(w_max, nj),
        in_specs=[
            pl.BlockSpec((MOE_CHUNK, d), lambda w, j, ce, cb, cn: (cb[w], 0)),
            pl.BlockSpec((None, None, d, tn),
                         lambda w, j, ce, cb, cn: (layer, ce[w], 0, col(w, j, cn))),
            pl.BlockSpec((None, None, d, tn),
                         lambda w, j, ce, cb, cn: (layer, ce[w], 0, col(w, j, cn) + nj)),
        ],
        out_specs=pl.BlockSpec((MOE_CHUNK, tn), lambda w, j, ce, cb, cn: (cb[w], col(w, j, cn))),
        scratch_shapes=[pltpu.VMEM((d, tn), BF16), pltpu.VMEM((d, tn), BF16)],
    )
    return pl.pallas_call(
        _gmm1_kernel,
        out_shape=jax.ShapeDtypeStruct((rows, f), BF16),
        grid_spec=grid_spec,
        compiler_params=_params(("arbitrary", "arbitrary")),
        name="moe_gmm1",
    )(plan["chunk_e"], plan["chunk_blk"], plan["chunk_n"], xs, w_gu, w_gu)


def _gmm2_kernel(re_ref, rb_ref, rn_ref, h_ref, w_ref, o_ref, wb_ref):
    n = rn_ref[pl.program_id(0)]
    kk = pl.program_id(2)

    @pl.when(n > 0)
    def _():
        @pl.when(kk == 0)
        def _():
            o_ref[...] = jnp.zeros_like(o_ref)

        wb_ref[...] = w_ref[...].astype(BF16)

        def blk(off, size):
            rows = pl.ds(off, size)
            o_ref[rows, :] += jnp.dot(h_ref[rows, :], wb_ref[...],
                                      preferred_element_type=F32)

        _for_each_row_block(n, ROW_SIZES_ROWBLK, blk)


def _moe_gmm2(hs, w_down, layer, plan):
    rows, f = hs.shape
    d = w_down.shape[3]
    tk = _pick_tile(f, MOE_DOWN_K, V7X_LANES)
    tn = _pick_tile(d, MOE_DOWN_COLS, V7X_LANES)
    nk = f // tk
    nn = d // tn
    r_max = plan["rb_blk"].shape[0]

    def kblk(r, kk, rn):
        return jnp.where(rn[r] > 0, kk, nk - 1)

    def nblk(r, j, rn):
        return jnp.where(rn[r] > 0, j, nn - 1)

    grid_spec = pltpu.PrefetchScalarGridSpec(
        num_scalar_prefetch=3,
        grid=(r_max, nn, nk),
        in_specs=[
            pl.BlockSpec((MOE_ROWBLK, tk),
                         lambda r, j, kk, re, rb, rn: (rb[r], kblk(r, kk, rn))),
            pl.BlockSpec((None, None, tk, tn),
                         lambda r, j, kk, re, rb, rn:
                         (layer, re[r], kblk(r, kk, rn), nblk(r, j, rn))),
        ],
        out_specs=pl.BlockSpec((MOE_ROWBLK, tn),
                               lambda r, j, kk, re, rb, rn: (rb[r], nblk(r, j, rn))),
        scratch_shapes=[pltpu.VMEM((tk, tn), BF16)],
    )
    return pl.pallas_call(
        _gmm2_kernel,
        out_shape=jax.ShapeDtypeStruct((rows, d), F32),
        grid_spec=grid_spec,
        compiler_params=_params(("arbitrary", "arbitrary", "arbitrary")),
        name="moe_gmm2",
    )(plan["rb_e"], plan["rb_blk"], plan["rb_n"], hs, w_down)


def _combine_kernel(*refs, alpha, emit_bf16, head_tiles):
    first_ref, next_ref, x_ref, route_ref, ys_hbm, g_ref, b_ref, y_ref = refs[:8]
    yb_ref = refs[8] if emit_bf16 else None
    tail_ref = refs[8] if head_tiles is not None else None
    buf_ref, sem = refs[-2:]
    i = pl.program_id(0)
    n_rows = x_ref.shape[0]
    slot = i % 2

    def start(idx_ref, s):
        for k in range(2):
            _start_rows(ys_hbm, idx_ref, 2, k, buf_ref.at[s, k], n_rows, sem.at[s])

    @pl.when(i == 0)
    def _():
        start(first_ref, 0)

    @pl.when(i + 1 < pl.num_programs(0))
    def _():
        start(next_ref, 1 - slot)

    for k in range(2):
        _wait_rows(ys_hbm, buf_ref.at[slot, k], n_rows, sem.at[slot])
    w1 = route_ref[:, 2:3]
    w2 = route_ref[:, 3:4]
    z = alpha * x_ref[...] + (w1 * buf_ref[slot, 0] + w2 * buf_ref[slot, 1])
    y = _layer_norm_rows(z, g_ref[...], b_ref[...])
    if head_tiles is None:
        y_ref[...] = y
    else:
        @pl.when(i < head_tiles)
        def _():
            y_ref[...] = y

        @pl.when(i >= head_tiles)
        def _():
            tail_ref[...] = y
    if emit_bf16:
        yb_ref[...] = y.astype(BF16)


def _moe_combine(x, route, ys, plan, ln_g, ln_b, ln_layer, alpha, emit_bf16, head_rows):
    n, d = x.shape
    tq = _pick_tile(n, COMBINE_ROWS, V7X_BF16_ROWS)
    nt = n // tq
    slots = plan["slot"].reshape(nt, 1, 2 * tq)
    smem_blk = lambda index_map: pl.BlockSpec((1, 1, 2 * tq), index_map,
                                              memory_space=pltpu.SMEM)
    head_tiles = None
    if head_rows is not None:
        assert not emit_bf16 and head_rows % tq == 0 and 0 < head_rows < n
        head_tiles = head_rows // tq
        out_shape = [jax.ShapeDtypeStruct((head_rows, d), F32),
                     jax.ShapeDtypeStruct((n - head_rows, d), F32)]
        out_specs = [pl.BlockSpec((tq, d), lambda i: (jnp.minimum(i, head_tiles - 1), 0)),
                     pl.BlockSpec((tq, d), lambda i: (jnp.maximum(i - head_tiles, 0), 0))]
    else:
        out_shape = [jax.ShapeDtypeStruct((n, d), F32)]
        out_specs = [pl.BlockSpec((tq, d), lambda i: (i, 0))]
    if emit_bf16:
        out_shape.append(jax.ShapeDtypeStruct((n, d), BF16))
        out_specs.append(pl.BlockSpec((tq, d), lambda i: (i, 0)))
    return pl.pallas_call(
        functools.partial(_combine_kernel, alpha=alpha, emit_bf16=emit_bf16,
                          head_tiles=head_tiles),
        out_shape=out_shape,
        grid=(nt,),
        in_specs=[
            smem_blk(lambda i: (0, 0, 0)),
            smem_blk(lambda i: (jnp.minimum(i + 1, nt - 1), 0, 0)),
            pl.BlockSpec((tq, d), lambda i: (i, 0)),
            pl.BlockSpec((tq, V7X_LANES), lambda i: (i, 0)),
            pl.BlockSpec(memory_space=pl.ANY),
            pl.BlockSpec((None, 1, d), lambda i: (ln_layer, 0, 0)),
            pl.BlockSpec((None, 1, d), lambda i: (ln_layer, 0, 0)),
        ],
        out_specs=out_specs,
        scratch_shapes=[pltpu.VMEM((2, 2, tq, d), F32), pltpu.SemaphoreType.DMA((2,))],
        compiler_params=_params(("arbitrary",)),
        name="moe_combine",
    )(slots, slots, x, route, ys, _row_vec(ln_g), _row_vec(ln_b))


def _moe_ffn(x, route, w_gu, w_down, layer, ln_g, ln_b, ln_layer, alpha, emit_bf16,
             head_rows):
    n = x.shape[0]
    ids = route[:, :2].astype(jnp.int32)
    plan = _route_plan(ids[:, 0], ids[:, 1], n)
    xs = _moe_gather(x, plan)
    hs = _moe_gmm1(xs, w_gu, layer, plan)
    ys = _moe_gmm2(hs, w_down, layer, plan)
    return _moe_combine(x, route, ys, plan, ln_g, ln_b, ln_layer, alpha, emit_bf16,
                        head_rows)


def kernel(x_prompt, x_sample, state_conv, cache_k, cache_v, ln1_g, ln1_b, ln2_g, ln2_b,
           w_pw1, b_pw1, w_dw, b_dw, conv_norm_g, conv_norm_b, w_pw2, b_pw2,
           w_qkv, attn_sinks, w_o, w_ffn_gu, w_ffn_down, w_router, w_exp_gu, w_exp_down):
    batch, seq, d_model = x_prompt.shape
    dec_batch, dec_seq, _ = x_sample.shape
    depth = ln1_g.shape[0]
    alpha = float((2 * depth) ** 0.25)
    hd = d_model // N_HEADS
    kvw = N_KV_HEADS * hd
    n_p = batch * seq
    assert seq >= CONV_HIST and seq % WINDOW == 0

    x = jnp.concatenate([x_prompt.reshape(n_p, d_model),
                         x_sample.reshape(dec_batch * dec_seq, d_model)], axis=0)
    xb = x.astype(BF16)
    conv_p, conv_s, k_p, v_p, k_s, v_s = [], [], [], [], [], []

    def tail_rows(a, n_rows):
        return jnp.stack([a[(bi + 1) * seq - n_rows:(bi + 1) * seq] for bi in range(batch)])

    w_pw2, w_o = w_pw2.astype(BF16), w_o.astype(BF16)

    for i in range(depth):
        j = i // 2
        last = i == depth - 1
        if i % 2 == 0:
            u = _mm1_pair(xb, w_pw1, j, b_pw1, "glu", F32, f"pw1_glu_{i}")
            u_s = u[n_p:].reshape(dec_batch, dec_seq, u.shape[1])
            ext_s = jnp.concatenate([state_conv[j].astype(F32), u_s], axis=1)
            conv_p.append(tail_rows(u, CONV_HIST))
            conv_s.append(ext_s[:, -CONV_HIST:])
            c = _conv_prompt(u, batch, seq, j, w_dw, b_dw, conv_norm_g, conv_norm_b)
            c = _conv_sample(ext_s, c, n_p, j, w_dw, b_dw, conv_norm_g, conv_norm_b)
            x1, x1b = _mm2(c, w_pw2, j, b_pw2, x, ln1_g, ln1_b, i, alpha, None, True,
                           f"pw2_ln_{i}")
            hmid = _mm1_pair(x1b, w_ffn_gu, j, None, "swiglu", BF16, f"ffn_gu_{i}")
            x, *rest = _mm2(hmid, w_ffn_down, j, None, x1, ln2_g, ln2_b, i, alpha, None,
                            not last, f"ffn_down_ln_{i}")
            xb = rest[0] if rest else None
        else:
            qkv = _mm1_plain(xb, w_qkv, j, F32, f"qkv_{i}")
            kv_p = tail_rows(qkv[:, d_model:], WINDOW)
            kv_s = qkv[n_p:, d_model:].reshape(dec_batch, dec_seq, 2 * kvw)
            heads = lambda a: a.reshape(a.shape[0], a.shape[1], N_KV_HEADS, hd)
            w_buf = cache_k.shape[2]
            k_buf = cache_k[j].astype(F32)
            v_buf = cache_v[j].astype(F32)
            k_p.append(heads(kv_p[..., :kvw]))
            v_p.append(heads(kv_p[..., kvw:]))
            k_s.append(jnp.concatenate([k_buf, heads(kv_s[..., :kvw])], axis=1)[:, -w_buf:])
            v_s.append(jnp.concatenate([v_buf, heads(kv_s[..., kvw:])], axis=1)[:, -w_buf:])
            o = _attn_prompt(qkv, attn_sinks[j], batch, seq, d_model)
            o = _attn_sample(qkv, o, n_p, dec_batch, dec_seq, k_buf, v_buf, attn_sinks[j],
                             d_model)
            x1, route = _mm2(o, w_o, j, None, x, ln1_g, ln1_b, i, alpha, w_router, False,
                             f"wo_ln_router_{i}")
            tq = _pick_tile(x1.shape[0], COMBINE_ROWS, V7X_BF16_ROWS)
            split = n_p if last and n_p % tq == 0 else None
            x, *rest = _moe_ffn(x1, route, w_exp_gu, w_exp_down, j, ln2_g, ln2_b, i, alpha,
                                not last, split)
            if split is not None:
                x = (x, rest[0])
            else:
                xb = rest[0] if rest else None

    x_p, x_s = x if isinstance(x, tuple) else (x[:n_p], x[n_p:])
    y_p = x_p.reshape(batch, seq, d_model)
    y_s = x_s.reshape(dec_batch, dec_seq, d_model)
    return (y_p, y_s, jnp.stack(conv_p), jnp.stack(conv_s),
            jnp.stack(k_p), jnp.stack(v_p), jnp.stack(k_s), jnp.stack(v_s))
```
